```python
import jax
import jax.numpy as jnp
from jax import lax
import numpy as np

D_MODEL = 2048
BATCH = 2
SEQ = 8192
DEPTH = 1

CONV_CHANNELS = D_MODEL // 2
CONV_TAPS = 31
HEAD_DIM = 128
ATTN_WIDTH = D_MODEL // 2
N_HEADS = ATTN_WIDTH // HEAD_DIM
MOBA_BLOCK = 256
MOBA_TOPK = 3
Q_CHUNK = 64
N_GROUPS = 8
EXPERTS_PER_GROUP = 8
N_EXPERTS = N_GROUPS * EXPERTS_PER_GROUP
EXPERT_TOPK = 2
D_EXPERT = D_MODEL // 4
MOE_BLOCK = 128
NORM_EPS = 1e-6
NEG_BIG = -1e30
IN_COLS = 2 * CONV_CHANNELS + 3 * ATTN_WIDTH + 2 * D_MODEL

kernel_name = 'hybrid_conformer_moba_hmoe_block'


def rms_norm(x, g):
    xf = x.astype(jnp.float32)
    y = xf * lax.rsqrt(jnp.mean(xf * xf, axis=-1, keepdims=True) + NORM_EPS)
    return (y * g.astype(jnp.float32)).astype(x.dtype)


def layer_norm(x, g, b):
    xf = x.astype(jnp.float32)
    mu = jnp.mean(xf, axis=-1, keepdims=True)
    xc = xf - mu
    y = xc * lax.rsqrt(jnp.mean(xc * xc, axis=-1, keepdims=True) + NORM_EPS)
    return (y * g.astype(jnp.float32) + b.astype(jnp.float32)).astype(x.dtype)


def alibi_slopes(n_heads):
    return jnp.exp2(-8.0 * jnp.arange(1, n_heads + 1, dtype=jnp.float32) / n_heads)


def conformer_conv(a, w_dw, b_dw, ln_g, ln_b, w_pw):
    val, gate = a[..., :CONV_CHANNELS], a[..., CONV_CHANNELS:]
    z = val * jax.nn.sigmoid(gate)
    z = lax.conv_general_dilated(
        z, w_dw[:, None, :], window_strides=(1,), padding=[(CONV_TAPS - 1, 0)],
        dimension_numbers=('NWC', 'WIO', 'NWC'),
        feature_group_count=CONV_CHANNELS) + b_dw
    z = jax.nn.silu(layer_norm(z, ln_g, ln_b))
    return z @ w_pw


def moba_attention(q, k, v):
    b, h, s, dh = q.shape
    nb = -(-s // MOBA_BLOCK)
    pad = nb * MOBA_BLOCK - s
    k_pad = jnp.pad(k, ((0, 0), (0, 0), (0, pad), (0, 0)))
    v_pad = jnp.pad(v, ((0, 0), (0, 0), (0, pad), (0, 0)))
    k_blk = k_pad.reshape(b, h, nb, MOBA_BLOCK, dh)
    v_blk = v_pad.reshape(b, h, nb, MOBA_BLOCK, dh)
    n_sel = min(MOBA_TOPK, nb)
    scale = dh ** -0.5
    slopes = alibi_slopes(h)

    k_mean = jnp.mean(k_blk.astype(jnp.float32), axis=3)
    gate = jnp.einsum('bhsd,bhnd->bhsn', q.astype(jnp.float32), k_mean)
    q_block = jnp.arange(s) // MOBA_BLOCK
    fully_past = jnp.arange(nb)[None, :] < q_block[:, None]
    gate = jnp.where(fully_past, gate, -jnp.inf)
    _, sel = lax.top_k(gate, n_sel)

    nc = s // Q_CHUNK
    qc = q.reshape(b, h, nc, Q_CHUNK, dh).transpose(2, 0, 1, 3, 4)
    selc = sel.reshape(b, h, nc, Q_CHUNK, n_sel).transpose(2, 0, 1, 3, 4)
    bi = jnp.arange(b)[:, None, None, None]
    hi = jnp.arange(h)[None, :, None, None]
    key_off = jnp.arange(MOBA_BLOCK)

    def chunk(args):
        c, qx, sx = args
        q_pos = c * Q_CHUNK + jnp.arange(Q_CHUNK)
        own = (c * Q_CHUNK) // MOBA_BLOCK
        ks = k_blk[bi, hi, sx]
        vs = v_blk[bi, hi, sx]
        s_sel = jnp.einsum('bhqd,bhqkld->bhqkl', qx, ks).astype(jnp.float32) * scale
        pos_sel = sx[..., None] * MOBA_BLOCK + key_off
        dist_sel = (q_pos[None, None, :, None, None] - pos_sel).astype(jnp.float32)
        s_sel = s_sel - slopes[None, :, None, None, None] * dist_sel
        s_sel = jnp.where((sx < own)[..., None], s_sel, NEG_BIG)
        s_sel = s_sel.reshape(b, h, Q_CHUNK, n_sel * MOBA_BLOCK)
        k_own = lax.dynamic_slice_in_dim(k_pad, own * MOBA_BLOCK, MOBA_BLOCK, axis=2)
        v_own = lax.dynamic_slice_in_dim(v_pad, own * MOBA_BLOCK, MOBA_BLOCK, axis=2)
        s_own = jnp.einsum('bhqd,bhld->bhql', qx, k_own).astype(jnp.float32) * scale
        dist_own = q_pos[:, None] - (own * MOBA_BLOCK + key_off)[None, :]
        s_own = s_own - slopes[None, :, None, None] * dist_own.astype(jnp.float32)
        s_own = jnp.where((dist_own >= 0)[None, None], s_own, NEG_BIG)
        p = jax.nn.softmax(jnp.concatenate([s_sel, s_own], axis=-1), axis=-1)
        p_sel = p[..., :n_sel * MOBA_BLOCK].reshape(b, h, Q_CHUNK, n_sel, MOBA_BLOCK).astype(v.dtype)
        p_own = p[..., n_sel * MOBA_BLOCK:].astype(v.dtype)
        return (jnp.einsum('bhqkl,bhqkld->bhqd', p_sel, vs)
                + jnp.einsum('bhql,bhld->bhqd', p_own, v_own))

    out = lax.map(chunk, (jnp.arange(nc), qc, selc))
    return out.transpose(1, 2, 0, 3, 4).reshape(b, h, s, dh)


def hierarchical_moe(xt, w_rg, b_rg, w_re, b_re, w_g, w_u, w_d):
    t, d = xt.shape
    g_prob = jax.nn.softmax((xt @ w_rg).astype(jnp.float32) + b_rg.astype(jnp.float32), axis=-1)
    g_p, g_idx = lax.top_k(g_prob, 1)
    e_logits = ((xt @ w_re).astype(jnp.float32) + b_re.astype(jnp.float32)).reshape(
        t, N_GROUPS, EXPERTS_PER_GROUP)
    e_logits = jnp.take_along_axis(e_logits, g_idx[:, :, None], axis=1)[:, 0]
    e_val, e_idx = lax.top_k(e_logits, EXPERT_TOPK)
    comb = (g_p * jax.nn.softmax(e_val, axis=-1)).reshape(-1)
    expert_id = (g_idx * EXPERTS_PER_GROUP + e_idx).reshape(-1)
    token_id = jnp.repeat(jnp.arange(t, dtype=jnp.int32), EXPERT_TOPK)

    n_assign = t * EXPERT_TOPK
    n_blocks = -(-n_assign // MOE_BLOCK) + N_EXPERTS
    n_slots = n_blocks * MOE_BLOCK
    order = jnp.argsort(expert_id)
    e_sorted = expert_id[order]
    counts = jnp.zeros((N_EXPERTS,), jnp.int32).at[expert_id].add(1)
    start = jnp.cumsum(counts) - counts
    padded = (counts + MOE_BLOCK - 1) // MOE_BLOCK * MOE_BLOCK
    pad_end = jnp.cumsum(padded)
    pad_start = pad_end - padded
    dest = pad_start[e_sorted] + jnp.arange(n_assign, dtype=jnp.int32) - start[e_sorted]
    slot_tok = jnp.zeros((n_slots,), jnp.int32).at[dest].set(token_id[order])
    slot_w = jnp.zeros((n_slots,), jnp.float32).at[dest].set(comb[order])
    block_expert = jnp.minimum(
        jnp.searchsorted(pad_end, jnp.arange(n_blocks, dtype=jnp.int32) * MOE_BLOCK, side='right'),
        N_EXPERTS - 1)

    def run_block(args):
        tok, e = args
        xb = xt[tok]
        hdn = jax.nn.silu(xb @ w_g[e]) * (xb @ w_u[e])
        return hdn @ w_d[e]

    y = lax.map(run_block, (slot_tok.reshape(n_blocks, MOE_BLOCK), block_expert))
    y = y.reshape(n_slots, d) * slot_w[:, None].astype(xt.dtype)
    return jnp.zeros_like(xt).at[slot_tok].add(y)


def setup_inputs(seed: int = 0) -> dict:
    key = jax.random.key(seed)
    ks = jax.random.split(key, 20)
    f32 = jnp.float32
    L = DEPTH

    def nrm(k, shape, fan_in):
        return jax.random.normal(k, shape, f32) * fan_in ** -0.5

    def small(k, shape, sc=0.02):
        return jax.random.normal(k, shape, f32) * sc

    return {
        'x': jax.random.normal(ks[0], (BATCH, SEQ, D_MODEL), f32),
        'norm1_g': 1.0 + small(ks[1], (L, D_MODEL)),
        'w_in': nrm(ks[2], (L, D_MODEL, IN_COLS), D_MODEL),
        'conv_dw_w': nrm(ks[3], (L, CONV_TAPS, CONV_CHANNELS), CONV_TAPS),
        'conv_dw_b': small(ks[4], (L, CONV_CHANNELS)),
        'conv_ln_g': 1.0 + small(ks[5], (L, CONV_CHANNELS)),
        'conv_ln_b': small(ks[6], (L, CONV_CHANNELS)),
        'w_conv_out': nrm(ks[7], (L, CONV_CHANNELS, D_MODEL), CONV_CHANNELS),
        'w_attn_out': nrm(ks[8], (L, ATTN_WIDTH, D_MODEL), ATTN_WIDTH),
        'gate_b': small(ks[9], (L, 2 * D_MODEL)),
        'w_out': nrm(ks[10], (L, D_MODEL, D_MODEL), D_MODEL),
        'norm2_g': 1.0 + small(ks[11], (L, D_MODEL)),
        'w_router_group': nrm(ks[12], (L, D_MODEL, N_GROUPS), D_MODEL),
        'b_router_group': small(ks[13], (L, N_GROUPS), 0.01),
        'w_router_expert': nrm(ks[14], (L, D_MODEL, N_EXPERTS), D_MODEL),
        'b_router_expert': small(ks[15], (L, N_EXPERTS), 0.01),
        'w_exp_gate': nrm(ks[16], (L, N_EXPERTS, D_MODEL, D_EXPERT), D_MODEL),
        'w_exp_up': nrm(ks[17], (L, N_EXPERTS, D_MODEL, D_EXPERT), D_MODEL),
        'w_exp_down': nrm(ks[18], (L, N_EXPERTS, D_EXPERT, D_MODEL), D_EXPERT),
        'norm_f_g': 1.0 + small(ks[19], (D_MODEL,)),
    }


def reference(x, norm1_g, w_in, conv_dw_w, conv_dw_b, conv_ln_g, conv_ln_b, w_conv_out,
              w_attn_out, gate_b, w_out, norm2_g, w_router_group, b_router_group,
              w_router_expert, b_router_expert, w_exp_gate, w_exp_up, w_exp_down, norm_f_g):
    b, s, d = x.shape
    c1 = 2 * CONV_CHANNELS
    c2 = c1 + 3 * ATTN_WIDTH
    h = x
    for l in range(DEPTH):
        u = rms_norm(h, norm1_g[l])
        proj = u @ w_in[l]
        a_conv, qkv, g_lin = proj[..., :c1], proj[..., c1:c2], proj[..., c2:]
        y_conv = conformer_conv(a_conv, conv_dw_w[l], conv_dw_b[l], conv_ln_g[l],
                                conv_ln_b[l], w_conv_out[l])
        qkv = qkv.reshape(b, s, 3, N_HEADS, HEAD_DIM).transpose(2, 0, 3, 1, 4)
        y_attn = moba_attention(qkv[0], qkv[1], qkv[2])
        y_attn = y_attn.transpose(0, 2, 1, 3).reshape(b, s, ATTN_WIDTH) @ w_attn_out[l]
        gates = jax.nn.sigmoid(g_lin + gate_b[l])
        merged = gates[..., :D_MODEL] * y_conv + gates[..., D_MODEL:] * y_attn
        h = h + merged @ w_out[l]
        hn = rms_norm(h, norm2_g[l]).reshape(b * s, d)
        y_moe = hierarchical_moe(hn, w_router_group[l], b_router_group[l], w_router_expert[l],
                                 b_router_expert[l], w_exp_gate[l], w_exp_up[l], w_exp_down[l])
        h = h + y_moe.reshape(b, s, d)
    return rms_norm(h, norm_f_g)
```

```python
import functools

import jax
import jax.numpy as jnp
from jax import lax
from jax.experimental import pallas as pl
from jax.experimental.pallas import tpu as pltpu

F32 = jnp.float32
BF16 = jnp.bfloat16

NORM_EPS = 1e-6
NEG_BIG = -1e30
CONV_TAPS = 31
HEAD_DIM = 128
MOBA_BLOCK = 256
MOBA_TOPK = 3
N_GROUPS = 8
EXPERTS_PER_GROUP = 8
N_EXPERTS = N_GROUPS * EXPERTS_PER_GROUP
MOE_BLOCK = 128
ROUTER_ROWS = 128
CONV_HALO = 32
V7X_VMEM_LIMIT_BYTES = 56 * 1024 * 1024

NT_DIMS = (((1,), (1,)), ((), ()))


def _params(semantics):
    return pltpu.CompilerParams(dimension_semantics=semantics, vmem_limit_bytes=V7X_VMEM_LIMIT_BYTES)


def _tile(n, want):
    t = min(n, want)
    while n % t:
        t //= 2
    return t


def _rmsnorm_kernel(x_ref, g_ref, o_ref):
    x = x_ref[...]
    ms = jnp.mean(x * x, axis=-1, keepdims=True)
    o_ref[...] = (x * lax.rsqrt(ms + NORM_EPS) * g_ref[...]).astype(o_ref.dtype)


def _rmsnorm(x, g, out_dtype):
    t, d = x.shape
    tm = _tile(t, 512)
    return pl.pallas_call(
        _rmsnorm_kernel,
        out_shape=jax.ShapeDtypeStruct((t, d), out_dtype),
        grid=(t // tm,),
        in_specs=[pl.BlockSpec((tm, d), lambda i: (i, 0)), pl.BlockSpec((1, d), lambda i: (0, 0))],
        out_specs=pl.BlockSpec((tm, d), lambda i: (i, 0)),
        compiler_params=_params(("parallel",)),
        name="rmsnorm",
    )(x, g.reshape(1, d))


def _glu_proj_kernel(u_ref, wv_ref, wg_ref, o_ref):
    u = u_ref[...]
    val = jnp.dot(u, wv_ref[...], preferred_element_type=F32)
    gate = jnp.dot(u, wg_ref[...], preferred_element_type=F32)
    o_ref[...] = val * jax.nn.sigmoid(gate)


def _proj_kernel(u_ref, w_ref, o_ref):
    o_ref[...] = jnp.dot(u_ref[...], w_ref[...], preferred_element_type=F32).astype(o_ref.dtype)


def _gate_proj_kernel(u_ref, w_ref, b_ref, o_ref):
    y = jnp.dot(u_ref[...], w_ref[...], preferred_element_type=F32) + b_ref[...]
    o_ref[...] = jax.nn.sigmoid(y).astype(o_ref.dtype)


def _proj_call(kernel, u, weights, extra, n_out, out_dtype, name):
    t, k = u.shape
    tm = _tile(t, 1024)
    tn = _tile(n_out, 512)
    in_specs = [pl.BlockSpec((tm, k), lambda i, j: (i, 0))]
    in_specs += [pl.BlockSpec((k, tn), lambda i, j: (0, j)) for _ in weights]
    in_specs += [pl.BlockSpec((1, tn), lambda i, j: (0, j)) for _ in extra]
    return pl.pallas_call(
        kernel,
        out_shape=jax.ShapeDtypeStruct((t, n_out), out_dtype),
        grid=(t // tm, n_out // tn),
        in_specs=in_specs,
        out_specs=pl.BlockSpec((tm, tn), lambda i, j: (i, j)),
        compiler_params=_params(("parallel", "parallel")),
        name=name,
    )(u, *weights, *extra)


def _conv_kernel(z_ref, halo_ref, w_ref, b_ref, g_ref, beta_ref, o_ref, ext_ref, acc_ref, *, tiles_per_seq):
    i = pl.program_id(0)
    ts, c = z_ref.shape
    seq_start = (i % tiles_per_seq) == 0
    ext_ref[0:CONV_HALO, :] = jnp.where(seq_start, 0.0, halo_ref[...])
    ext_ref[CONV_HALO:, :] = z_ref[...]
    rows = 64
    first = CONV_HALO - (CONV_TAPS - 1)
    for lc in range(c // 128):
        ls = slice(lc * 128, (lc + 1) * 128)
        for rc in range(ts // rows):
            acc = jnp.broadcast_to(b_ref[:, ls], (rows, 128))
            for tap in range(CONV_TAPS):
                lo = rc * rows + first + tap
                acc = acc + ext_ref[lo:lo + rows, ls] * w_ref[tap:tap + 1, ls]
            acc_ref[rc * rows:(rc + 1) * rows, ls] = acc
    y = acc_ref[...]
    mu = jnp.mean(y, axis=-1, keepdims=True)
    yc = y - mu
    var = jnp.mean(yc * yc, axis=-1, keepdims=True)
    yn = yc * lax.rsqrt(var + NORM_EPS) * g_ref[...] + beta_ref[...]
    o_ref[...] = (yn * jax.nn.sigmoid(yn)).astype(o_ref.dtype)


def _conv_branch(z, seq, w_dw, b_dw, ln_g, ln_b):
    t, c = z.shape
    ts = _tile(seq, 256)
    hb = ts // CONV_HALO
    kernel = functools.partial(_conv_kernel, tiles_per_seq=seq // ts)
    row = lambda v: v.reshape(1, c)
    return pl.pallas_call(
        kernel,
        out_shape=jax.ShapeDtypeStruct((t, c), BF16),
        grid=(t // ts,),
        in_specs=[
            pl.BlockSpec((ts, c), lambda i: (i, 0)),
            pl.BlockSpec((CONV_HALO, c), lambda i: (jnp.maximum(i * hb - 1, 0), 0)),
            pl.BlockSpec((CONV_TAPS, c), lambda i: (0, 0)),
            pl.BlockSpec((1, c), lambda i: (0, 0)),
            pl.BlockSpec((1, c), lambda i: (0, 0)),
            pl.BlockSpec((1, c), lambda i: (0, 0)),
        ],
        out_specs=pl.BlockSpec((ts, c), lambda i: (i, 0)),
        scratch_shapes=[pltpu.VMEM((ts + CONV_HALO, c), F32), pltpu.VMEM((ts, c), F32)],
        compiler_params=_params(("parallel",)),
        name="conv",
    )(z, z, w_dw, row(b_dw), row(ln_g), row(ln_b))


def _moba_kernel(q_ref, k_ref, v_ref, o_ref, kmean_ref, vt_ref, sel_ref, *, n_blk, n_heads, scale):
    head = pl.program_id(1)
    qb = pl.program_id(2)
    blk = MOBA_BLOCK
    slope = jnp.exp2(-8.0 * (head + 1).astype(F32) / n_heads)

    @pl.when(qb == 0)
    def _():
        def per_block(kb, carry):
            st = pl.multiple_of(kb * blk, blk)
            kmean_ref[pl.ds(kb, 1), :] = jnp.mean(k_ref[pl.ds(st, blk), :].astype(F32), axis=0, keepdims=True)
            vt_ref[:, pl.ds(st, blk)] = v_ref[pl.ds(st, blk), :].astype(F32).T.astype(BF16)
            return carry
        lax.fori_loop(0, n_blk, per_block, 0)

    q = q_ref[...]
    gate = lax.dot_general(kmean_ref[...].astype(BF16), q, NT_DIMS, preferred_element_type=F32)
    rows = lax.broadcasted_iota(jnp.int32, (n_blk, blk), 0)
    gate = jnp.where(rows < qb, gate, -jnp.inf)
    sel = jnp.zeros((n_blk, blk), F32)
    for _ in range(MOBA_TOPK):
        top = jnp.max(gate, axis=0, keepdims=True)
        idx = jnp.min(jnp.where(gate == top, rows, n_blk), axis=0, keepdims=True)
        pick = rows == idx
        sel = jnp.where(pick, jnp.where(top > -jnp.inf, 1.0, sel), sel)
        gate = jnp.where(pick, -jnp.inf, gate)
    sel_ref[...] = sel

    k_local = lax.broadcasted_iota(jnp.int32, (blk, blk), 0)
    q_local = lax.broadcasted_iota(jnp.int32, (blk, blk), 1)
    dist_own = q_local - k_local

    def scores(kb):
        st = pl.multiple_of(kb * blk, blk)
        s = lax.dot_general(k_ref[pl.ds(st, blk), :], q, NT_DIMS, preferred_element_type=F32)
        dist = dist_own + (qb - kb) * blk
        return s * scale - slope * dist.astype(F32), st

    s, st = scores(qb)
    s = jnp.where(dist_own >= 0, s, NEG_BIG)
    m = jnp.max(s, axis=0, keepdims=True)
    p = jnp.exp(s - m)
    l = jnp.sum(p, axis=0, keepdims=True)
    acc = jnp.dot(vt_ref[:, pl.ds(st, blk)], p.astype(BF16), preferred_element_type=F32)

    def per_past_block(kb, carry):
        m, l, acc = carry
        s, st = scores(kb)
        s = jnp.where(sel_ref[pl.ds(kb, 1), :] > 0.5, s, NEG_BIG)
        m_new = jnp.maximum(m, jnp.max(s, axis=0, keepdims=True))
        alpha = jnp.exp(m - m_new)
        p = jnp.exp(s - m_new)
        l = alpha * l + jnp.sum(p, axis=0, keepdims=True)
        acc = alpha * acc + jnp.dot(vt_ref[:, pl.ds(st, blk)], p.astype(BF16), preferred_element_type=F32)
        return m_new, l, acc

    m, l, acc = lax.fori_loop(0, qb, per_past_block, (m, l, acc))
    o_ref[...] = (acc / l).T.astype(o_ref.dtype)


def _moba(qkv, batch, seq, n_heads):
    t = qkv.shape[0]
    blk = MOBA_BLOCK
    n_blk = seq // blk
    kernel = functools.partial(_moba_kernel, n_blk=n_blk, n_heads=n_heads, scale=HEAD_DIM ** -0.5)
    return pl.pallas_call(
        kernel,
        out_shape=jax.ShapeDtypeStruct((t, n_heads * HEAD_DIM), BF16),
        grid=(batch, n_heads, n_blk),
        in_specs=[
            pl.BlockSpec((blk, HEAD_DIM), lambda b, h, i: (b * n_blk + i, h)),
            pl.BlockSpec((seq, HEAD_DIM), lambda b, h, i: (b, n_heads + h)),
            pl.BlockSpec((seq, HEAD_DIM), lambda b, h, i: (b, 2 * n_heads + h)),
        ],
        out_specs=pl.BlockSpec((blk, HEAD_DIM), lambda b, h, i: (b * n_blk + i, h)),
        scratch_shapes=[
            pltpu.VMEM((n_blk, HEAD_DIM), F32),
            pltpu.VMEM((HEAD_DIM, seq), BF16),
            pltpu.VMEM((n_blk, blk), F32),
        ],
        compiler_params=_params(("parallel", "parallel", "arbitrary")),
        name="moba",
    )(qkv, qkv, qkv)


def _mix_kernel(c_ref, a_ref, gates_ref, x_ref, wpw_ref, wao_ref, wout_ref, g2_ref, wrt_ref, br_ref,
                h_ref, route_ref, cnt_ref, carry_ref):
    i = pl.program_id(0)
    tm, d = x_ref.shape

    @pl.when(i == 0)
    def _():
        carry_ref[...] = jnp.zeros_like(carry_ref)

    yc = jnp.dot(c_ref[...], wpw_ref[...], preferred_element_type=F32)
    ya = jnp.dot(a_ref[...], wao_ref[...], preferred_element_type=F32)
    merged = gates_ref[:, :d].astype(F32) * yc + gates_ref[:, d:].astype(F32) * ya
    h = x_ref[...] + jnp.dot(merged.astype(BF16), wout_ref[...], preferred_element_type=F32)
    h_ref[...] = h
    hn = h * lax.rsqrt(jnp.mean(h * h, axis=-1, keepdims=True) + NORM_EPS) * g2_ref[...]
    logits = lax.dot_general(wrt_ref[...], hn.astype(BF16), NT_DIMS, preferred_element_type=F32) + br_ref[...]

    iota8 = lax.broadcasted_iota(jnp.int32, (N_GROUPS, tm), 0)
    gl = logits[0:N_GROUPS]
    gmax = jnp.max(gl, axis=0, keepdims=True)
    g_p = 1.0 / jnp.sum(jnp.exp(gl - gmax), axis=0, keepdims=True)
    g_idx = jnp.min(jnp.where(gl == gmax, iota8, N_GROUPS), axis=0, keepdims=True)
    esel = jnp.zeros((EXPERTS_PER_GROUP, tm), F32)
    for g in range(N_GROUPS):
        lo = N_GROUPS + g * EXPERTS_PER_GROUP
        esel = jnp.where(g_idx == g, logits[lo:lo + EXPERTS_PER_GROUP], esel)
    v1 = jnp.max(esel, axis=0, keepdims=True)
    i1 = jnp.min(jnp.where(esel == v1, iota8, EXPERTS_PER_GROUP), axis=0, keepdims=True)
    rest = jnp.where(iota8 == i1, -jnp.inf, esel)
    v2 = jnp.max(rest, axis=0, keepdims=True)
    i2 = jnp.min(jnp.where(rest == v2, iota8, EXPERTS_PER_GROUP), axis=0, keepdims=True)
    r = jnp.exp(v2 - v1)
    w1 = g_p / (1.0 + r)
    w2 = g_p * r / (1.0 + r)
    e1 = g_idx * EXPERTS_PER_GROUP + i1
    e2 = g_idx * EXPERTS_PER_GROUP + i2

    iota_e = lax.broadcasted_iota(jnp.int32, (N_EXPERTS, tm), 0)
    oh1 = iota_e == e1
    oh2 = iota_e == e2
    onehot = jnp.concatenate([jnp.where(oh1, 1.0, 0.0), jnp.where(oh2, 1.0, 0.0)], axis=1)
    before = (lax.broadcasted_iota(jnp.int32, (2 * tm, 2 * tm), 0)
              < lax.broadcasted_iota(jnp.int32, (2 * tm, 2 * tm), 1))
    prefix = jnp.dot(onehot.astype(BF16), jnp.where(before, 1.0, 0.0).astype(BF16), preferred_element_type=F32)
    carry = carry_ref[...]
    base = prefix + carry[:, 0:1]
    rank1 = jnp.sum(jnp.where(oh1, base[:, :tm], 0.0), axis=0, keepdims=True)
    rank2 = jnp.sum(jnp.where(oh2, base[:, tm:], 0.0), axis=0, keepdims=True)
    total = carry + jnp.sum(onehot, axis=1, keepdims=True)
    carry_ref[...] = total
    cnt_ref[...] = total

    route_ref[0:1, :] = e1.astype(F32)
    route_ref[1:2, :] = e2.astype(F32)
    route_ref[2:3, :] = w1
    route_ref[3:4, :] = w2
    route_ref[4:5, :] = rank1
    route_ref[5:6, :] = rank2
    route_ref[6:8, :] = jnp.zeros((2, tm), F32)


def _mix(c, a, gates, x, w_pw, w_ao, w_out, g2, wr_t, b_r):
    t, d = x.shape
    cw = c.shape[1]
    aw = a.shape[1]
    tm = _tile(t, 256)
    const = lambda shape: pl.BlockSpec(shape, lambda i: (0, 0), pipeline_mode=pl.Buffered(1))
    return pl.pallas_call(
        _mix_kernel,
        out_shape=(
            jax.ShapeDtypeStruct((t, d), F32),
            jax.ShapeDtypeStruct((8, t), F32),
            jax.ShapeDtypeStruct((N_EXPERTS, 128), F32),
        ),
        grid=(t // tm,),
        in_specs=[
            pl.BlockSpec((tm, cw), lambda i: (i, 0)),
            pl.BlockSpec((tm, aw), lambda i: (i, 0)),
            pl.BlockSpec((tm, 2 * d), lambda i: (i, 0)),
            pl.BlockSpec((tm, d), lambda i: (i, 0)),
            const((cw, d)),
            const((aw, d)),
            const((d, d)),
            const((1, d)),
            const((ROUTER_ROWS, d)),
            const((ROUTER_ROWS, 1)),
        ],
        out_specs=(
            pl.BlockSpec((tm, d), lambda i: (i, 0)),
            pl.BlockSpec((8, tm), lambda i: (0, i)),
            pl.BlockSpec((N_EXPERTS, 128), lambda i: (0, 0)),
        ),
        scratch_shapes=[pltpu.VMEM((N_EXPERTS, 128), F32)],
        compiler_params=_params(("arbitrary",)),
        name="mix",
    )(c, a, gates, x, w_pw, w_ao, w_out, g2.reshape(1, d), wr_t, b_r)


def _row_copy(src_hbm, row, dst_vmem, dst_row, sem):
    return pltpu.make_async_copy(src_hbm.at[pl.ds(row, 1), :], dst_vmem.at[pl.ds(dst_row, 1), :], sem)


def _experts_kernel(blk_expert_ref, slot_tok_ref, n_valid_ref,
                    h_hbm, g2_ref, wg_ref, wu_ref, wd_ref, y_ref,
                    xbuf, sems, wg_b, wu_b, wd_b):
    n = pl.program_id(0)
    n_valid = n_valid_ref[0]
    rows = MOE_BLOCK

    def gather(block, slot):
        return [_row_copy(h_hbm, slot_tok_ref[block * rows + r], xbuf.at[slot], r, sems.at[slot])
                for r in range(rows)]

    @pl.when(jnp.logical_and(n == 0, n_valid > 0))
    def _():
        for cp in gather(0, 0):
            cp.start()

    @pl.when(n + 1 < n_valid)
    def _():
        for cp in gather(n + 1, (n + 1) % 2):
            cp.start()

    @pl.when(n >= n_valid)
    def _():
        y_ref[...] = jnp.zeros_like(y_ref)

    @pl.when(n < n_valid)
    def _():
        expert_changed = jnp.logical_or(n == 0, blk_expert_ref[n] != blk_expert_ref[jnp.maximum(n - 1, 0)])

        @pl.when(expert_changed)
        def _():
            wg_b[...] = wg_ref[...].astype(BF16)
            wu_b[...] = wu_ref[...].astype(BF16)
            wd_b[...] = wd_ref[...].astype(BF16)

        slot = n % 2
        for cp in gather(n, slot):
            cp.wait()
        x = xbuf[slot]
        xn = x * lax.rsqrt(jnp.mean(x * x, axis=-1, keepdims=True) + NORM_EPS) * g2_ref[...]
        xb = xn.astype(BF16)
        gate = jnp.dot(xb, wg_b[...], preferred_element_type=F32)
        up = jnp.dot(xb, wu_b[...], preferred_element_type=F32)
        hidden = (gate * jax.nn.sigmoid(gate) * up).astype(BF16)
        y_ref[...] = jnp.dot(hidden, wd_b[...], preferred_element_type=F32)


def _experts(h, g2, w_g, w_u, w_d, blk_expert, slot_tok, n_valid):
    t, d = h.shape
    de = w_g.shape[2]
    n_blocks = blk_expert.shape[0]
    rows = MOE_BLOCK
    grid_spec = pltpu.PrefetchScalarGridSpec(
        num_scalar_prefetch=3,
        grid=(n_blocks,),
        in_specs=[
            pl.BlockSpec(memory_space=pl.ANY),
            pl.BlockSpec((1, d), lambda n, be, st, nv: (0, 0)),
            pl.BlockSpec((None, d, de), lambda n, be, st, nv: (be[n], 0, 0)),
            pl.BlockSpec((None, d, de), lambda n, be, st, nv: (be[n], 0, 0)),
            pl.BlockSpec((None, de, d), lambda n, be, st, nv: (be[n], 0, 0)),
        ],
        out_specs=pl.BlockSpec((rows, d), lambda n, be, st, nv: (n, 0)),
        scratch_shapes=[
            pltpu.VMEM((2, rows, d), F32),
            pltpu.SemaphoreType.DMA((2,)),
            pltpu.VMEM((d, de), BF16),
            pltpu.VMEM((d, de), BF16),
            pltpu.VMEM((de, d), BF16),
        ],
    )
    return pl.pallas_call(
        _experts_kernel,
        out_shape=jax.ShapeDtypeStruct((n_blocks * rows, d), F32),
        grid_spec=grid_spec,
        compiler_params=_params(("arbitrary",)),
        name="experts",
    )(blk_expert, slot_tok, n_valid, h, g2.reshape(1, d), w_g, w_u, w_d)


def _combine_kernel(slot1_ref, slot2_ref, h_ref, w_ref, gf_ref, y_hbm, o_ref, ybuf, sems):
    i = pl.program_id(0)
    n_tiles = pl.num_programs(0)
    tm = h_ref.shape[0]

    def gather(tile, buf):
        cps = []
        for r in range(tm):
            cps.append(_row_copy(y_hbm, slot1_ref[tile * tm + r], ybuf.at[buf, 0], r, sems.at[buf]))
            cps.append(_row_copy(y_hbm, slot2_ref[tile * tm + r], ybuf.at[buf, 1], r, sems.at[buf]))
        return cps

    @pl.when(i == 0)
    def _():
        for cp in gather(0, 0):
            cp.start()

    @pl.when(i + 1 < n_tiles)
    def _():
        for cp in gather(i + 1, (i + 1) % 2):
            cp.start()

    buf = i % 2
    for cp in gather(i, buf):
        cp.wait()
    w = w_ref[...]
    h = h_ref[...] + w[:, 0:1] * ybuf[buf, 0] + w[:, 1:2] * ybuf[buf, 1]
    o_ref[...] = h * lax.rsqrt(jnp.mean(h * h, axis=-1, keepdims=True) + NORM_EPS) * gf_ref[...]


def _combine(h, y, slot1, slot2, w, gf):
    t, d = h.shape
    tm = _tile(t, 128)
    grid_spec = pltpu.PrefetchScalarGridSpec(
        num_scalar_prefetch=2,
        grid=(t // tm,),
        in_specs=[
            pl.BlockSpec((tm, d), lambda i, s1, s2: (i, 0)),
            pl.BlockSpec((tm, 2), lambda i, s1, s2: (i, 0)),
            pl.BlockSpec((1, d), lambda i, s1, s2: (0, 0)),
            pl.BlockSpec(memory_space=pl.ANY),
        ],
        out_specs=pl.BlockSpec((tm, d), lambda i, s1, s2: (i, 0)),
        scratch_shapes=[pltpu.VMEM((2, 2, tm, d), F32), pltpu.SemaphoreType.DMA((2,))],
    )
    return pl.pallas_call(
        _combine_kernel,
        out_shape=jax.ShapeDtypeStruct((t, d), F32),
        grid_spec=grid_spec,
        compiler_params=_params(("arbitrary",)),
        name="combine",
    )(slot1, slot2, h, w, gf.reshape(1, d), y)


def _layer(h, batch, seq, norm1_g, w_in, conv_dw_w, conv_dw_b, conv_ln_g, conv_ln_b, w_conv_out, w_attn_out,
           gate_b, w_out, norm2_g, w_rg, b_rg, w_re, b_re, w_exp_gate, w_exp_up, w_exp_down):
    t, d = h.shape
    c = w_conv_out.shape[0]
    aw = w_attn_out.shape[0]
    n_heads = aw // HEAD_DIM
    c1 = 2 * c
    c2 = c1 + 3 * aw

    u = _rmsnorm(h, norm1_g, BF16)
    w_in_b = w_in.astype(BF16)
    z = _proj_call(_glu_proj_kernel, u, [w_in_b[:, :c], w_in_b[:, c:c1]], [], c, F32, "glu_proj")
    qkv = _proj_call(_proj_kernel, u, [w_in_b[:, c1:c2]], [], 3 * aw, BF16, "qkv_proj")
    gates = _proj_call(_gate_proj_kernel, u, [w_in_b[:, c2:]], [gate_b.reshape(1, 2 * d)], 2 * d, BF16, "gate_proj")

    conv = _conv_branch(z, seq, conv_dw_w, conv_dw_b, conv_ln_g, conv_ln_b)
    attn = _moba(qkv, batch, seq, n_heads)

    wr_t = jnp.zeros((ROUTER_ROWS, d), F32).at[:N_GROUPS].set(w_rg.T).at[N_GROUPS:N_GROUPS + N_EXPERTS].set(w_re.T)
    b_r = jnp.zeros((ROUTER_ROWS,), F32).at[:N_GROUPS].set(b_rg).at[N_GROUPS:N_GROUPS + N_EXPERTS].set(b_re)
    h_mid, route, counts = _mix(conv, attn, gates, h, w_conv_out.astype(BF16), w_attn_out.astype(BF16),
                                w_out.astype(BF16), norm2_g, wr_t.astype(BF16), b_r.reshape(ROUTER_ROWS, 1))

    n_assign = 2 * t
    n_blocks = -(-n_assign // MOE_BLOCK) + N_EXPERTS
    counts = counts[:, 0].astype(jnp.int32)
    padded = (counts + MOE_BLOCK - 1) // MOE_BLOCK * MOE_BLOCK
    pad_end = jnp.cumsum(padded)
    pad_start = pad_end - padded
    e1 = route[0].astype(jnp.int32)
    e2 = route[1].astype(jnp.int32)
    slot1 = pad_start[e1] + route[4].astype(jnp.int32)
    slot2 = pad_start[e2] + route[5].astype(jnp.int32)
    tok = jnp.arange(t, dtype=jnp.int32)
    slot_tok = jnp.zeros((n_blocks * MOE_BLOCK,), jnp.int32).at[slot1].set(tok).at[slot2].set(tok)
    blk_expert = jnp.minimum(
        jnp.searchsorted(pad_end, jnp.arange(n_blocks, dtype=jnp.int32) * MOE_BLOCK, side='right'),
        N_EXPERTS - 1).astype(jnp.int32)
    n_valid = (pad_end[-1:] // MOE_BLOCK).astype(jnp.int32)

    y = _experts(h_mid, norm2_g, w_exp_gate, w_exp_up, w_exp_down, blk_expert, slot_tok, n_valid)
    weights = jnp.stack([route[2], route[3]], axis=1)
    return h_mid, y, slot1, slot2, weights


def kernel(x, norm1_g, w_in, conv_dw_w, conv_dw_b, conv_ln_g, conv_ln_b, w_conv_out, w_attn_out, gate_b, w_out,
           norm2_g, w_router_group, b_router_group, w_router_expert, b_router_expert, w_exp_gate, w_exp_up,
           w_exp_down, norm_f_g):
    b, s, d = x.shape
    assert norm1_g.shape[0] == 1, "single-layer stack: the combine kernel fuses the final RMSNorm"
    h_mid, y, slot1, slot2, weights = _layer(
        x.reshape(b * s, d), b, s, norm1_g[0], w_in[0], conv_dw_w[0], conv_dw_b[0], conv_ln_g[0], conv_ln_b[0],
        w_conv_out[0], w_attn_out[0], gate_b[0], w_out[0], norm2_g[0], w_router_group[0],
        b_router_group[0], w_router_expert[0], b_router_expert[0], w_exp_gate[0], w_exp_up[0], w_exp_down[0])
    return _combine(h_mid, y, slot1, slot2, weights, norm_f_g).reshape(b, s, d)
```

```python
import functools

import jax
import jax.numpy as jnp
from jax import lax
from jax.experimental import pallas as pl
from jax.experimental.pallas import tpu as pltpu

F32 = jnp.float32
BF16 = jnp.bfloat16

NORM_EPS = 1e-6
NEG_BIG = -1e30
MASKED = -2e30
LOG2_E = 1.4426950408889634
MOBA_GROUP = 4
MOBA_HEADS_PER_STEP = 4
CONV_TAPS = 31
HEAD_DIM = 128
MOBA_BLOCK = 256
MOBA_TOPK = 3
N_GROUPS = 8
EXPERTS_PER_GROUP = 8
N_EXPERTS = N_GROUPS * EXPERTS_PER_GROUP
MOE_BLOCK = 128
ROUTER_ROWS = 128
CONV_HALO = 32
V7X_VMEM_LIMIT_BYTES = 56 * 1024 * 1024

NT_DIMS = (((1,), (1,)), ((), ()))


def _params(semantics):
    return pltpu.CompilerParams(dimension_semantics=semantics, vmem_limit_bytes=V7X_VMEM_LIMIT_BYTES)


def _tile(n, want):
    t = min(n, want)
    while n % t:
        t //= 2
    return t


def _rmsnorm_kernel(x_ref, g_ref, o_ref):
    x = x_ref[...]
    ms = jnp.mean(x * x, axis=-1, keepdims=True)
    o_ref[...] = (x * lax.rsqrt(ms + NORM_EPS) * g_ref[...]).astype(o_ref.dtype)


def _rmsnorm(x, g, out_dtype):
    t, d = x.shape
    tm = _tile(t, 512)
    return pl.pallas_call(
        _rmsnorm_kernel,
        out_shape=jax.ShapeDtypeStruct((t, d), out_dtype),
        grid=(t // tm,),
        in_specs=[pl.BlockSpec((tm, d), lambda i: (i, 0)), pl.BlockSpec((1, d), lambda i: (0, 0))],
        out_specs=pl.BlockSpec((tm, d), lambda i: (i, 0)),
        compiler_params=_params(("parallel",)),
        name="rmsnorm",
    )(x, g.reshape(1, d))


def _glu_proj_kernel(u_ref, wv_ref, wg_ref, o_ref):
    u = u_ref[...]
    val = jnp.dot(u, wv_ref[...], preferred_element_type=F32)
    gate = jnp.dot(u, wg_ref[...], preferred_element_type=F32)
    o_ref[...] = val * jax.nn.sigmoid(gate)


def _proj_kernel(u_ref, w_ref, o_ref):
    o_ref[...] = jnp.dot(u_ref[...], w_ref[...], preferred_element_type=F32).astype(o_ref.dtype)


def _gate_proj_kernel(u_ref, w_ref, b_ref, o_ref):
    y = jnp.dot(u_ref[...], w_ref[...], preferred_element_type=F32) + b_ref[...]
    o_ref[...] = jax.nn.sigmoid(y).astype(o_ref.dtype)


def _proj_call(kernel, u, weights, extra, n_out, out_dtype, name):
    t, k = u.shape
    tm = _tile(t, 1024)
    tn = _tile(n_out, 512)
    in_specs = [pl.BlockSpec((tm, k), lambda i, j: (i, 0))]
    in_specs += [pl.BlockSpec((k, tn), lambda i, j: (0, j)) for _ in weights]
    in_specs += [pl.BlockSpec((1, tn), lambda i, j: (0, j)) for _ in extra]
    return pl.pallas_call(
        kernel,
        out_shape=jax.ShapeDtypeStruct((t, n_out), out_dtype),
        grid=(t // tm, n_out // tn),
        in_specs=in_specs,
        out_specs=pl.BlockSpec((tm, tn), lambda i, j: (i, j)),
        compiler_params=_params(("parallel", "parallel")),
        name=name,
    )(u, *weights, *extra)


def _conv_kernel(z_ref, halo_ref, w_ref, b_ref, g_ref, beta_ref, o_ref, ext_ref, acc_ref, *, tiles_per_seq):
    i = pl.program_id(0)
    ts, c = z_ref.shape
    seq_start = (i % tiles_per_seq) == 0
    ext_ref[0:CONV_HALO, :] = jnp.where(seq_start, 0.0, halo_ref[...])
    ext_ref[CONV_HALO:, :] = z_ref[...]
    rows = 64
    first = CONV_HALO - (CONV_TAPS - 1)
    for lc in range(c // 128):
        ls = slice(lc * 128, (lc + 1) * 128)
        for rc in range(ts // rows):
            acc = jnp.broadcast_to(b_ref[:, ls], (rows, 128))
            for tap in range(CONV_TAPS):
                lo = rc * rows + first + tap
                acc = acc + ext_ref[lo:lo + rows, ls] * w_ref[tap:tap + 1, ls]
            acc_ref[rc * rows:(rc + 1) * rows, ls] = acc
    y = acc_ref[...]
    mu = jnp.mean(y, axis=-1, keepdims=True)
    yc = y - mu
    var = jnp.mean(yc * yc, axis=-1, keepdims=True)
    yn = yc * lax.rsqrt(var + NORM_EPS) * g_ref[...] + beta_ref[...]
    o_ref[...] = (yn * jax.nn.sigmoid(yn)).astype(o_ref.dtype)


def _conv_branch(z, seq, w_dw, b_dw, ln_g, ln_b):
    t, c = z.shape
    ts = _tile(seq, 256)
    hb = ts // CONV_HALO
    kernel = functools.partial(_conv_kernel, tiles_per_seq=seq // ts)
    row = lambda v: v.reshape(1, c)
    return pl.pallas_call(
        kernel,
        out_shape=jax.ShapeDtypeStruct((t, c), BF16),
        grid=(t // ts,),
        in_specs=[
            pl.BlockSpec((ts, c), lambda i: (i, 0)),
            pl.BlockSpec((CONV_HALO, c), lambda i: (jnp.maximum(i * hb - 1, 0), 0)),
            pl.BlockSpec((CONV_TAPS, c), lambda i: (0, 0)),
            pl.BlockSpec((1, c), lambda i: (0, 0)),
            pl.BlockSpec((1, c), lambda i: (0, 0)),
            pl.BlockSpec((1, c), lambda i: (0, 0)),
        ],
        out_specs=pl.BlockSpec((ts, c), lambda i: (i, 0)),
        scratch_shapes=[pltpu.VMEM((ts + CONV_HALO, c), F32), pltpu.VMEM((ts, c), F32)],
        compiler_params=_params(("parallel",)),
        name="conv",
    )(z, z, w_dw, row(b_dw), row(ln_g), row(ln_b))


def _moba_kernel(q_ref, k_ref, v_ref, o_ref, kmean_ref, vt_ref, sel_ref, bias_ref, *, n_blk, n_heads, hp, scale):
    head0 = pl.program_id(1) * hp
    qb = pl.program_id(2)
    blk = MOBA_BLOCK
    dh = HEAD_DIM
    lanes = [slice(h * dh, (h + 1) * dh) for h in range(hp)]
    slope2 = [jnp.exp2(-8.0 * (head0 + h + 1).astype(F32) / n_heads) * LOG2_E for h in range(hp)]

    @pl.when(qb == 0)
    def _():
        def per_block(kb, carry):
            st = pl.multiple_of(kb * blk, blk)
            for h in range(hp):
                kmean_ref[h, pl.ds(kb, 1), :] = jnp.mean(
                    k_ref[pl.ds(st, blk), lanes[h]].astype(F32), axis=0, keepdims=True)
                vt_ref[h, :, pl.ds(st, blk)] = v_ref[pl.ds(st, blk), lanes[h]].astype(F32).T.astype(BF16)
            return carry
        lax.fori_loop(0, n_blk, per_block, 0)
        k_local = lax.broadcasted_iota(jnp.int32, (blk, blk), 0)
        q_local = lax.broadcasted_iota(jnp.int32, (blk, blk), 1)
        for h in range(hp):
            bias = -slope2[h] * (q_local - k_local).astype(F32)
            bias_ref[h, 0] = bias
            bias_ref[h, 1] = jnp.where(q_local >= k_local, bias, MASKED)

    rows = lax.broadcasted_iota(jnp.int32, (n_blk, blk), 0)
    qs = []
    for h in range(hp):
        q = q_ref[:, lanes[h]]
        gate = lax.dot_general(kmean_ref[h].astype(BF16), q, NT_DIMS, preferred_element_type=F32)
        gate = jnp.where(rows < qb, gate, -jnp.inf)
        sel = jnp.where(rows == qb, 1.0, 0.0)
        for _ in range(MOBA_TOPK):
            top = jnp.max(gate, axis=0, keepdims=True)
            idx = jnp.min(jnp.where(gate == top, rows, n_blk), axis=0, keepdims=True)
            pick = rows == idx
            sel = jnp.where(pick, jnp.where(top > -jnp.inf, 1.0, sel), sel)
            gate = jnp.where(pick, -jnp.inf, gate)
        sel_ref[h] = sel
        qs.append((q.astype(F32) * (scale * LOG2_E)).astype(BF16))

    def per_group(j, carry):
        starts = [pl.multiple_of((j * MOBA_GROUP + g) * blk, blk) for g in range(MOBA_GROUP)]
        chains = [(h, j * MOBA_GROUP + g, starts[g]) for h in range(hp) for g in range(MOBA_GROUP)]
        scores = [lax.dot_general(k_ref[pl.ds(st, blk), lanes[h]], qs[h], NT_DIMS, preferred_element_type=F32)
                  for h, _, st in chains]
        stats = []
        for (h, kb, _), s in zip(chains, scores):
            t = s + bias_ref[h, (kb == qb).astype(jnp.int32)]
            mb = jnp.max(t, axis=0, keepdims=True)
            p = jnp.exp2(t - mb)
            lb = jnp.sum(p, axis=0, keepdims=True)
            block_term = slope2[h] * ((kb - qb) * blk).astype(F32)
            mb = jnp.where(sel_ref[h, pl.ds(kb, 1), :] > 0.5, mb + block_term, MASKED)
            stats.append((mb, lb, p.astype(BF16)))
        pvs = [jnp.dot(vt_ref[h, :, pl.ds(st, blk)], p, preferred_element_type=F32)
               for (h, _, st), (_, _, p) in zip(chains, stats)]
        out = []
        for h in range(hp):
            m, l, acc = carry[h]
            mine = [(mb, lb, pv) for (hh, _, _), (mb, lb, _), pv in zip(chains, stats, pvs) if hh == h]
            m_new = m
            for mb, _, _ in mine:
                m_new = jnp.maximum(m_new, mb)
            w = jnp.exp2(m - m_new)
            l = w * l
            acc = w * acc
            for mb, lb, pv in mine:
                w = jnp.exp2(mb - m_new)
                l = l + w * lb
                acc = acc + w * pv
            out.append((m_new, l, acc))
        return tuple(out)

    init = tuple((jnp.full((1, blk), NEG_BIG, F32), jnp.zeros((1, blk), F32), jnp.zeros((dh, blk), F32))
                 for _ in range(hp))
    n_groups = qb // MOBA_GROUP + 1
    final = lax.fori_loop(0, n_groups, per_group, init)
    for h in range(hp):
        _, l, acc = final[h]
        o_ref[:, lanes[h]] = (acc / l).T.astype(o_ref.dtype)


def _moba(qkv, batch, seq, n_heads):
    t = qkv.shape[0]
    blk = MOBA_BLOCK
    n_blk = seq // blk
    hp = _tile(n_heads, MOBA_HEADS_PER_STEP)
    assert n_blk % MOBA_GROUP == 0
    kernel = functools.partial(_moba_kernel, n_blk=n_blk, n_heads=n_heads, hp=hp, scale=HEAD_DIM ** -0.5)
    wide = hp * HEAD_DIM
    hsteps = n_heads // hp
    return pl.pallas_call(
        kernel,
        out_shape=jax.ShapeDtypeStruct((t, n_heads * HEAD_DIM), BF16),
        grid=(batch, hsteps, n_blk),
        in_specs=[
            pl.BlockSpec((blk, wide), lambda b, h, i: (b * n_blk + i, h)),
            pl.BlockSpec((seq, wide), lambda b, h, i: (b, hsteps + h)),
            pl.BlockSpec((seq, wide), lambda b, h, i: (b, 2 * hsteps + h)),
        ],
        out_specs=pl.BlockSpec((blk, wide), lambda b, h, i: (b * n_blk + i, h)),
        scratch_shapes=[
            pltpu.VMEM((hp, n_blk, HEAD_DIM), F32),
            pltpu.VMEM((hp, HEAD_DIM, seq), BF16),
            pltpu.VMEM((hp, n_blk, blk), F32),
            pltpu.VMEM((hp, 2, blk, blk), F32),
        ],
        compiler_params=_params(("parallel", "parallel", "arbitrary")),
        name="moba",
    )(qkv, qkv, qkv)


def _mix_kernel(c_ref, a_ref, gates_ref, x_ref, wpw_ref, wao_ref, wout_ref, g2_ref, wrt_ref, br_ref,
                h_ref, route_ref, cnt_ref, carry_ref):
    i = pl.program_id(0)
    tm, d = x_ref.shape

    @pl.when(i == 0)
    def _():
        carry_ref[...] = jnp.zeros_like(carry_ref)

    yc = jnp.dot(c_ref[...], wpw_ref[...], preferred_element_type=F32)
    ya = jnp.dot(a_ref[...], wao_ref[...], preferred_element_type=F32)
    merged = gates_ref[:, :d].astype(F32) * yc + gates_ref[:, d:].astype(F32) * ya
    h = x_ref[...] + jnp.dot(merged.astype(BF16), wout_ref[...], preferred_element_type=F32)
    h_ref[...] = h
    hn = h * lax.rsqrt(jnp.mean(h * h, axis=-1, keepdims=True) + NORM_EPS) * g2_ref[...]
    logits = lax.dot_general(wrt_ref[...], hn.astype(BF16), NT_DIMS, preferred_element_type=F32) + br_ref[...]

    iota8 = lax.broadcasted_iota(jnp.int32, (N_GROUPS, tm), 0)
    gl = logits[0:N_GROUPS]
    gmax = jnp.max(gl, axis=0, keepdims=True)
    g_p = 1.0 / jnp.sum(jnp.exp(gl - gmax), axis=0, keepdims=True)
    g_idx = jnp.min(jnp.where(gl == gmax, iota8, N_GROUPS), axis=0, keepdims=True)
    esel = jnp.zeros((EXPERTS_PER_GROUP, tm), F32)
    for g in range(N_GROUPS):
        lo = N_GROUPS + g * EXPERTS_PER_GROUP
        esel = jnp.where(g_idx == g, logits[lo:lo + EXPERTS_PER_GROUP], esel)
    v1 = jnp.max(esel, axis=0, keepdims=True)
    i1 = jnp.min(jnp.where(esel == v1, iota8, EXPERTS_PER_GROUP), axis=0, keepdims=True)
    rest = jnp.where(iota8 == i1, -jnp.inf, esel)
    v2 = jnp.max(rest, axis=0, keepdims=True)
    i2 = jnp.min(jnp.where(rest == v2, iota8, EXPERTS_PER_GROUP), axis=0, keepdims=True)
    r = jnp.exp(v2 - v1)
    w1 = g_p / (1.0 + r)
    w2 = g_p * r / (1.0 + r)
    e1 = g_idx * EXPERTS_PER_GROUP + i1
    e2 = g_idx * EXPERTS_PER_GROUP + i2

    iota_e = lax.broadcasted_iota(jnp.int32, (N_EXPERTS, tm), 0)
    oh1 = iota_e == e1
    oh2 = iota_e == e2
    onehot = jnp.concatenate([jnp.where(oh1, 1.0, 0.0), jnp.where(oh2, 1.0, 0.0)], axis=1)
    before = (lax.broadcasted_iota(jnp.int32, (2 * tm, 2 * tm), 0)
              < lax.broadcasted_iota(jnp.int32, (2 * tm, 2 * tm), 1))
    prefix = jnp.dot(onehot.astype(BF16), jnp.where(before, 1.0, 0.0).astype(BF16), preferred_element_type=F32)
    carry = carry_ref[...]
    base = prefix + carry[:, 0:1]
    rank1 = jnp.sum(jnp.where(oh1, base[:, :tm], 0.0), axis=0, keepdims=True)
    rank2 = jnp.sum(jnp.where(oh2, base[:, tm:], 0.0), axis=0, keepdims=True)
    total = carry + jnp.sum(onehot, axis=1, keepdims=True)
    carry_ref[...] = total
    cnt_ref[...] = total

    route_ref[0:1, :] = e1.astype(F32)
    route_ref[1:2, :] = e2.astype(F32)
    route_ref[2:3, :] = w1
    route_ref[3:4, :] = w2
    route_ref[4:5, :] = rank1
    route_ref[5:6, :] = rank2
    route_ref[6:8, :] = jnp.zeros((2, tm), F32)


def _mix(c, a, gates, x, w_pw, w_ao, w_out, g2, wr_t, b_r):
    t, d = x.shape
    cw = c.shape[1]
    aw = a.shape[1]
    tm = _tile(t, 256)
    const = lambda shape: pl.BlockSpec(shape, lambda i: (0, 0), pipeline_mode=pl.Buffered(1))
    return pl.pallas_call(
        _mix_kernel,
        out_shape=(
            jax.ShapeDtypeStruct((t, d), F32),
            jax.ShapeDtypeStruct((8, t), F32),
            jax.ShapeDtypeStruct((N_EXPERTS, 128), F32),
        ),
        grid=(t // tm,),
        in_specs=[
            pl.BlockSpec((tm, cw), lambda i: (i, 0)),
            pl.BlockSpec((tm, aw), lambda i: (i, 0)),
            pl.BlockSpec((tm, 2 * d), lambda i: (i, 0)),
            pl.BlockSpec((tm, d), lambda i: (i, 0)),
            const((cw, d)),
            const((aw, d)),
            const((d, d)),
            const((1, d)),
            const((ROUTER_ROWS, d)),
            const((ROUTER_ROWS, 1)),
        ],
        out_specs=(
            pl.BlockSpec((tm, d), lambda i: (i, 0)),
            pl.BlockSpec((8, tm), lambda i: (0, i)),
            pl.BlockSpec((N_EXPERTS, 128), lambda i: (0, 0)),
        ),
        scratch_shapes=[pltpu.VMEM((N_EXPERTS, 128), F32)],
        compiler_params=_params(("arbitrary",)),
        name="mix",
    )(c, a, gates, x, w_pw, w_ao, w_out, g2.reshape(1, d), wr_t, b_r)


def _row_copy(src_hbm, row, dst_vmem, dst_row, sem):
    return pltpu.make_async_copy(src_hbm.at[pl.ds(row, 1), :], dst_vmem.at[pl.ds(dst_row, 1), :], sem)


def _experts_kernel(blk_expert_ref, slot_tok_ref, n_valid_ref,
                    h_hbm, g2_ref, wg_ref, wu_ref, wd_ref, y_ref,
                    xbuf, sems, wg_b, wu_b, wd_b):
    n = pl.program_id(0)
    n_valid = n_valid_ref[0]
    rows = MOE_BLOCK

    def gather(block, slot):
        return [_row_copy(h_hbm, slot_tok_ref[block * rows + r], xbuf.at[slot], r, sems.at[slot])
                for r in range(rows)]

    @pl.when(jnp.logical_and(n == 0, n_valid > 0))
    def _():
        for cp in gather(0, 0):
            cp.start()

    @pl.when(n + 1 < n_valid)
    def _():
        for cp in gather(n + 1, (n + 1) % 2):
            cp.start()

    @pl.when(n >= n_valid)
    def _():
        y_ref[...] = jnp.zeros_like(y_ref)

    @pl.when(n < n_valid)
    def _():
        expert_changed = jnp.logical_or(n == 0, blk_expert_ref[n] != blk_expert_ref[jnp.maximum(n - 1, 0)])

        @pl.when(expert_changed)
        def _():
            wg_b[...] = wg_ref[...].astype(BF16)
            wu_b[...] = wu_ref[...].astype(BF16)
            wd_b[...] = wd_ref[...].astype(BF16)

        slot = n % 2
        for cp in gather(n, slot):
            cp.wait()
        x = xbuf[slot]
        xn = x * lax.rsqrt(jnp.mean(x * x, axis=-1, keepdims=True) + NORM_EPS) * g2_ref[...]
        xb = xn.astype(BF16)
        gate = jnp.dot(xb, wg_b[...], preferred_element_type=F32)
        up = jnp.dot(xb, wu_b[...], preferred_element_type=F32)
        hidden = (gate * jax.nn.sigmoid(gate) * up).astype(BF16)
        y_ref[...] = jnp.dot(hidden, wd_b[...], preferred_element_type=F32)


def _experts(h, g2, w_g, w_u, w_d, blk_expert, slot_tok, n_valid):
    t, d = h.shape
    de = w_g.shape[2]
    n_blocks = blk_expert.shape[0]
    rows = MOE_BLOCK
    grid_spec = pltpu.PrefetchScalarGridSpec(
        num_scalar_prefetch=3,
        grid=(n_blocks,),
        in_specs=[
            pl.BlockSpec(memory_space=pl.ANY),
            pl.BlockSpec((1, d), lambda n, be, st, nv: (0, 0)),
            pl.BlockSpec((None, d, de), lambda n, be, st, nv: (be[n], 0, 0)),
            pl.BlockSpec((None, d, de), lambda n, be, st, nv: (be[n], 0, 0)),
            pl.BlockSpec((None, de, d), lambda n, be, st, nv: (be[n], 0, 0)),
        ],
        out_specs=pl.BlockSpec((rows, d), lambda n, be, st, nv: (n, 0)),
        scratch_shapes=[
            pltpu.VMEM((2, rows, d), F32),
            pltpu.SemaphoreType.DMA((2,)),
            pltpu.VMEM((d, de), BF16),
            pltpu.VMEM((d, de), BF16),
            pltpu.VMEM((de, d), BF16),
        ],
    )
    return pl.pallas_call(
        _experts_kernel,
        out_shape=jax.ShapeDtypeStruct((n_blocks * rows, d), F32),
        grid_spec=grid_spec,
        compiler_params=_params(("arbitrary",)),
        name="experts",
    )(blk_expert, slot_tok, n_valid, h, g2.reshape(1, d), w_g, w_u, w_d)


def _combine_kernel(slot1_ref, slot2_ref, h_ref, w_ref, gf_ref, y_hbm, o_ref, ybuf, sems):
    i = pl.program_id(0)
    n_tiles = pl.num_programs(0)
    tm = h_ref.shape[0]

    def gather(tile, buf):
        cps = []
        for r in range(tm):
            cps.append(_row_copy(y_hbm, slot1_ref[tile * tm + r], ybuf.at[buf, 0], r, sems.at[buf]))
            cps.append(_row_copy(y_hbm, slot2_ref[tile * tm + r], ybuf.at[buf, 1], r, sems.at[buf]))
        return cps

    @pl.when(i == 0)
    def _():
        for cp in gather(0, 0):
            cp.start()

    @pl.when(i + 1 < n_tiles)
    def _():
        for cp in gather(i + 1, (i + 1) % 2):
            cp.start()

    buf = i % 2
    for cp in gather(i, buf):
        cp.wait()
    w = w_ref[...]
    h = h_ref[...] + w[:, 0:1] * ybuf[buf, 0] + w[:, 1:2] * ybuf[buf, 1]
    o_ref[...] = h * lax.rsqrt(jnp.mean(h * h, axis=-1, keepdims=True) + NORM_EPS) * gf_ref[...]


def _combine(h, y, slot1, slot2, w, gf):
    t, d = h.shape
    tm = _tile(t, 128)
    grid_spec = pltpu.PrefetchScalarGridSpec(
        num_scalar_prefetch=2,
        grid=(t // tm,),
        in_specs=[
            pl.BlockSpec((tm, d), lambda i, s1, s2: (i, 0)),
            pl.BlockSpec((tm, 2), lambda i, s1, s2: (i, 0)),
            pl.BlockSpec((1, d), lambda i, s1, s2: (0, 0)),
            pl.BlockSpec(memory_space=pl.ANY),
        ],
        out_specs=pl.BlockSpec((tm, d), lambda i, s1, s2: (i, 0)),
        scratch_shapes=[pltpu.VMEM((2, 2, tm, d), F32), pltpu.SemaphoreType.DMA((2,))],
    )
    return pl.pallas_call(
        _combine_kernel,
        out_shape=jax.ShapeDtypeStruct((t, d), F32),
        grid_spec=grid_spec,
        compiler_params=_params(("arbitrary",)),
        name="combine",
    )(slot1, slot2, h, w, gf.reshape(1, d), y)


def _layer(h, batch, seq, norm1_g, w_in, conv_dw_w, conv_dw_b, conv_ln_g, conv_ln_b, w_conv_out, w_attn_out,
           gate_b, w_out, norm2_g, w_rg, b_rg, w_re, b_re, w_exp_gate, w_exp_up, w_exp_down):
    t, d = h.shape
    c = w_conv_out.shape[0]
    aw = w_attn_out.shape[0]
    n_heads = aw // HEAD_DIM
    c1 = 2 * c
    c2 = c1 + 3 * aw

    u = _rmsnorm(h, norm1_g, BF16)
    w_in_b = w_in.astype(BF16)
    z = _proj_call(_glu_proj_kernel, u, [w_in_b[:, :c], w_in_b[:, c:c1]], [], c, F32, "glu_proj")
    qkv = _proj_call(_proj_kernel, u, [w_in_b[:, c1:c2]], [], 3 * aw, BF16, "qkv_proj")
    gates = _proj_call(_gate_proj_kernel, u, [w_in_b[:, c2:]], [gate_b.reshape(1, 2 * d)], 2 * d, BF16, "gate_proj")

    conv = _conv_branch(z, seq, conv_dw_w, conv_dw_b, conv_ln_g, conv_ln_b)
    attn = _moba(qkv, batch, seq, n_heads)

    wr_t = jnp.zeros((ROUTER_ROWS, d), F32).at[:N_GROUPS].set(w_rg.T).at[N_GROUPS:N_GROUPS + N_EXPERTS].set(w_re.T)
    b_r = jnp.zeros((ROUTER_ROWS,), F32).at[:N_GROUPS].set(b_rg).at[N_GROUPS:N_GROUPS + N_EXPERTS].set(b_re)
    h_mid, route, counts = _mix(conv, attn, gates, h, w_conv_out.astype(BF16), w_attn_out.astype(BF16),
                                w_out.astype(BF16), norm2_g, wr_t.astype(BF16), b_r.reshape(ROUTER_ROWS, 1))

    n_assign = 2 * t
    n_blocks = -(-n_assign // MOE_BLOCK) + N_EXPERTS
    counts = counts[:, 0].astype(jnp.int32)
    padded = (counts + MOE_BLOCK - 1) // MOE_BLOCK * MOE_BLOCK
    pad_end = jnp.cumsum(padded)
    pad_start = pad_end - padded
    e1 = route[0].astype(jnp.int32)
    e2 = route[1].astype(jnp.int32)
    slot1 = pad_start[e1] + route[4].astype(jnp.int32)
    slot2 = pad_start[e2] + route[5].astype(jnp.int32)
    tok = jnp.arange(t, dtype=jnp.int32)
    slot_tok = jnp.zeros((n_blocks * MOE_BLOCK,), jnp.int32).at[slot1].set(tok).at[slot2].set(tok)
    blk_start = jnp.arange(n_blocks, dtype=jnp.int32) * MOE_BLOCK
    blk_expert = jnp.minimum(jnp.sum(pad_end[None, :] <= blk_start[:, None], axis=1), N_EXPERTS - 1).astype(jnp.int32)
    n_valid = (pad_end[-1:] // MOE_BLOCK).astype(jnp.int32)

    y = _experts(h_mid, norm2_g, w_exp_gate, w_exp_up, w_exp_down, blk_expert, slot_tok, n_valid)
    weights = jnp.stack([route[2], route[3]], axis=1)
    return h_mid, y, slot1, slot2, weights


def kernel(x, norm1_g, w_in, conv_dw_w, conv_dw_b, conv_ln_g, conv_ln_b, w_conv_out, w_attn_out, gate_b, w_out,
           norm2_g, w_router_group, b_router_group, w_router_expert, b_router_expert, w_exp_gate, w_exp_up,
           w_exp_down, norm_f_g):
    b, s, d = x.shape
    assert norm1_g.shape[0] == 1, "single-layer stack: the combine kernel fuses the final RMSNorm"
    h_mid, y, slot1, slot2, weights = _layer(
        x.reshape(b * s, d), b, s, norm1_g[0], w_in[0], conv_dw_w[0], conv_dw_b[0], conv_ln_g[0], conv_ln_b[0],
        w_conv_out[0], w_attn_out[0], gate_b[0], w_out[0], norm2_g[0], w_router_group[0],
        b_router_group[0], w_router_expert[0], b_router_expert[0], w_exp_gate[0], w_exp_up[0], w_exp_down[0])
    return _combine(h_mid, y, slot1, slot2, weights, norm_f_g).reshape(b, s, d)
```

```python
import functools

import jax
import jax.numpy as jnp
from jax import lax
from jax.experimental import pallas as pl
from jax.experimental.pallas import tpu as pltpu

F32 = jnp.float32
BF16 = jnp.bfloat16

NORM_EPS = 1e-6
NEG_BIG = -1e30
MASKED = -2e30
LOG2_E = 1.4426950408889634
MOBA_GROUP = 4
MOBA_HEADS_PER_STEP = 4
MOBA_BIAS_PIECES = 3
MOBA_SUM_ROWS = 8
CONV_TAPS = 31
HEAD_DIM = 128
MOBA_BLOCK = 256
MOBA_TOPK = 3
N_GROUPS = 8
EXPERTS_PER_GROUP = 8
N_EXPERTS = N_GROUPS * EXPERTS_PER_GROUP
MOE_BLOCK = 128
ROUTER_ROWS = 128
CONV_HALO = 32
V7X_VMEM_LIMIT_BYTES = 56 * 1024 * 1024

NT_DIMS = (((1,), (1,)), ((), ()))


def _params(semantics, flags=None):
    return pltpu.CompilerParams(dimension_semantics=semantics, vmem_limit_bytes=V7X_VMEM_LIMIT_BYTES, flags=flags)


def _tile(n, want):
    t = min(n, want)
    while n % t:
        t //= 2
    return t


def _rmsnorm_kernel(x_ref, g_ref, o_ref):
    x = x_ref[...]
    ms = jnp.mean(x * x, axis=-1, keepdims=True)
    o_ref[...] = (x * lax.rsqrt(ms + NORM_EPS) * g_ref[...]).astype(o_ref.dtype)


def _rmsnorm(x, g, out_dtype):
    t, d = x.shape
    tm = _tile(t, 512)
    return pl.pallas_call(
        _rmsnorm_kernel,
        out_shape=jax.ShapeDtypeStruct((t, d), out_dtype),
        grid=(t // tm,),
        in_specs=[pl.BlockSpec((tm, d), lambda i: (i, 0)), pl.BlockSpec((1, d), lambda i: (0, 0))],
        out_specs=pl.BlockSpec((tm, d), lambda i: (i, 0)),
        compiler_params=_params(("parallel",)),
        name="rmsnorm",
    )(x, g.reshape(1, d))


def _glu_proj_kernel(u_ref, wv_ref, wg_ref, o_ref):
    u = u_ref[...]
    val = jnp.dot(u, wv_ref[...], preferred_element_type=F32)
    gate = jnp.dot(u, wg_ref[...], preferred_element_type=F32)
    o_ref[...] = val * jax.nn.sigmoid(gate)


def _proj_kernel(u_ref, w_ref, o_ref):
    o_ref[...] = jnp.dot(u_ref[...], w_ref[...], preferred_element_type=F32).astype(o_ref.dtype)


def _gate_proj_kernel(u_ref, w_ref, b_ref, o_ref):
    y = jnp.dot(u_ref[...], w_ref[...], preferred_element_type=F32) + b_ref[...]
    o_ref[...] = jax.nn.sigmoid(y).astype(o_ref.dtype)


def _proj_call(kernel, u, weights, extra, n_out, out_dtype, name):
    t, k = u.shape
    tm = _tile(t, 1024)
    tn = _tile(n_out, 512)
    in_specs = [pl.BlockSpec((tm, k), lambda i, j: (i, 0))]
    in_specs += [pl.BlockSpec((k, tn), lambda i, j: (0, j)) for _ in weights]
    in_specs += [pl.BlockSpec((1, tn), lambda i, j: (0, j)) for _ in extra]
    return pl.pallas_call(
        kernel,
        out_shape=jax.ShapeDtypeStruct((t, n_out), out_dtype),
        grid=(t // tm, n_out // tn),
        in_specs=in_specs,
        out_specs=pl.BlockSpec((tm, tn), lambda i, j: (i, j)),
        compiler_params=_params(("parallel", "parallel")),
        name=name,
    )(u, *weights, *extra)


def _conv_kernel(z_ref, halo_ref, w_ref, b_ref, g_ref, beta_ref, o_ref, ext_ref, acc_ref, *, tiles_per_seq):
    i = pl.program_id(0)
    ts, c = z_ref.shape
    seq_start = (i % tiles_per_seq) == 0
    ext_ref[0:CONV_HALO, :] = jnp.where(seq_start, 0.0, halo_ref[...])
    ext_ref[CONV_HALO:, :] = z_ref[...]
    rows = 64
    first = CONV_HALO - (CONV_TAPS - 1)
    for lc in range(c // 128):
        ls = slice(lc * 128, (lc + 1) * 128)
        for rc in range(ts // rows):
            acc = jnp.broadcast_to(b_ref[:, ls], (rows, 128))
            for tap in range(CONV_TAPS):
                lo = rc * rows + first + tap
                acc = acc + ext_ref[lo:lo + rows, ls] * w_ref[tap:tap + 1, ls]
            acc_ref[rc * rows:(rc + 1) * rows, ls] = acc
    y = acc_ref[...]
    mu = jnp.mean(y, axis=-1, keepdims=True)
    yc = y - mu
    var = jnp.mean(yc * yc, axis=-1, keepdims=True)
    yn = yc * lax.rsqrt(var + NORM_EPS) * g_ref[...] + beta_ref[...]
    o_ref[...] = (yn * jax.nn.sigmoid(yn)).astype(o_ref.dtype)


def _conv_branch(z, seq, w_dw, b_dw, ln_g, ln_b):
    t, c = z.shape
    ts = _tile(seq, 256)
    hb = ts // CONV_HALO
    kernel = functools.partial(_conv_kernel, tiles_per_seq=seq // ts)
    row = lambda v: v.reshape(1, c)
    return pl.pallas_call(
        kernel,
        out_shape=jax.ShapeDtypeStruct((t, c), BF16),
        grid=(t // ts,),
        in_specs=[
            pl.BlockSpec((ts, c), lambda i: (i, 0)),
            pl.BlockSpec((CONV_HALO, c), lambda i: (jnp.maximum(i * hb - 1, 0), 0)),
            pl.BlockSpec((CONV_TAPS, c), lambda i: (0, 0)),
            pl.BlockSpec((1, c), lambda i: (0, 0)),
            pl.BlockSpec((1, c), lambda i: (0, 0)),
            pl.BlockSpec((1, c), lambda i: (0, 0)),
        ],
        out_specs=pl.BlockSpec((ts, c), lambda i: (i, 0)),
        scratch_shapes=[pltpu.VMEM((ts + CONV_HALO, c), F32), pltpu.VMEM((ts, c), F32)],
        compiler_params=_params(("parallel",)),
        name="conv",
    )(z, z, w_dw, row(b_dw), row(ln_g), row(ln_b))


def _moba_kernel(q_ref, k_ref, v_ref, o_ref, kmean_ref, kaug_ref, vt_ref, sel_ref, mask_ref,
                 *, n_blk, n_heads, hp, scale):
    head0 = pl.program_id(1) * hp
    qb = pl.program_id(2)
    blk = MOBA_BLOCK
    dh = HEAD_DIM
    lanes = [slice(h * dh, (h + 1) * dh) for h in range(hp)]
    slope2 = [jnp.exp2(-8.0 * (head0 + h + 1).astype(F32) / n_heads) * LOG2_E for h in range(hp)]

    @pl.when(qb == 0)
    def _():
        k_local = lax.broadcasted_iota(jnp.int32, (blk, dh), 0).astype(F32)
        col = lax.broadcasted_iota(jnp.int32, (blk, dh), 1)
        extras = []
        for h in range(hp):
            rest = slope2[h] * k_local
            extra = jnp.zeros((blk, dh), F32)
            for piece in range(MOBA_BIAS_PIECES):
                part = rest.astype(BF16).astype(F32)
                extra = jnp.where(col == piece, part, extra)
                rest = rest - part
            extras.append(extra.astype(BF16))

        def per_block(kb, carry):
            st = pl.multiple_of(kb * blk, blk)
            for h in range(hp):
                k = k_ref[pl.ds(st, blk), lanes[h]]
                kmean_ref[h, pl.ds(kb, 1), :] = jnp.mean(k.astype(F32), axis=0, keepdims=True)
                kaug_ref[h, pl.ds(st, blk), 0:dh] = k
                kaug_ref[h, pl.ds(st, blk), dh:2 * dh] = extras[h]
                vt_ref[h, 0:dh, pl.ds(st, blk)] = v_ref[pl.ds(st, blk), lanes[h]].astype(F32).T.astype(BF16)
            return carry
        lax.fori_loop(0, n_blk, per_block, 0)
        ones_row = lax.broadcasted_iota(jnp.int32, (MOBA_SUM_ROWS, vt_ref.shape[2]), 0) == 0
        for h in range(hp):
            vt_ref[h, dh:dh + MOBA_SUM_ROWS, :] = jnp.where(ones_row, 1.0, 0.0).astype(BF16)
        key_i = lax.broadcasted_iota(jnp.int32, (blk, blk), 0)
        query_i = lax.broadcasted_iota(jnp.int32, (blk, blk), 1)
        mask_ref[...] = jnp.where(query_i >= key_i, 0.0, MASKED)

    rows = lax.broadcasted_iota(jnp.int32, (n_blk, blk), 0)
    ones_cols = jnp.where(lax.broadcasted_iota(jnp.int32, (blk, dh), 1) < MOBA_BIAS_PIECES, 1.0, 0.0).astype(BF16)
    q_aug = []
    for h in range(hp):
        q = q_ref[:, lanes[h]]
        gate = lax.dot_general(kmean_ref[h].astype(BF16), q, NT_DIMS, preferred_element_type=F32)
        gate = jnp.where(rows < qb, gate, -jnp.inf)
        sel = jnp.zeros((n_blk, blk), F32)
        for _ in range(MOBA_TOPK):
            top = jnp.max(gate, axis=0, keepdims=True)
            idx = jnp.min(jnp.where(gate == top, rows, n_blk), axis=0, keepdims=True)
            pick = rows == idx
            sel = jnp.where(pick, jnp.where(top > -jnp.inf, 1.0, sel), sel)
            gate = jnp.where(pick, -jnp.inf, gate)
        sel_ref[h] = sel
        qs = (q.astype(F32) * (scale * LOG2_E)).astype(BF16)
        q_aug.append(jnp.concatenate([qs, ones_cols], axis=1))

    def block_scores(h, st):
        return lax.dot_general(kaug_ref[h, pl.ds(st, blk), :], q_aug[h], NT_DIMS, preferred_element_type=F32)

    own_st = pl.multiple_of(qb * blk, blk)
    own = [block_scores(h, own_st) + mask_ref[...] for h in range(hp)]
    carry = []
    for h in range(hp):
        m = jnp.max(own[h], axis=0, keepdims=True)
        p = jnp.exp2(own[h] - m).astype(BF16)
        carry.append((m, jnp.dot(vt_ref[h, :, pl.ds(own_st, blk)], p, preferred_element_type=F32)))

    def per_group(j, carry):
        starts = [pl.multiple_of((j * MOBA_GROUP + g) * blk, blk) for g in range(MOBA_GROUP)]
        chains = [(h, j * MOBA_GROUP + g, starts[g]) for h in range(hp) for g in range(MOBA_GROUP)]
        scores = [block_scores(h, st) for h, _, st in chains]
        stats = []
        for (h, kb, _), s in zip(chains, scores):
            mb = jnp.max(s, axis=0, keepdims=True)
            p = jnp.exp2(s - mb).astype(BF16)
            block_term = slope2[h] * ((kb - qb) * blk).astype(F32)
            mb = jnp.where(sel_ref[h, pl.ds(kb, 1), :] > 0.5, mb + block_term, MASKED)
            stats.append((mb, p))
        pvs = [jnp.dot(vt_ref[h, :, pl.ds(st, blk)], p, preferred_element_type=F32)
               for (h, _, st), (_, p) in zip(chains, stats)]
        out = []
        for h in range(hp):
            m, acc = carry[h]
            mine = [(mb, pv) for (hh, _, _), (mb, _), pv in zip(chains, stats, pvs) if hh == h]
            m_new = m
            for mb, _ in mine:
                m_new = jnp.maximum(m_new, mb)
            acc = jnp.exp2(m - m_new) * acc
            for mb, pv in mine:
                acc = acc + jnp.exp2(mb - m_new) * pv
            out.append((m_new, acc))
        return tuple(out)

    n_groups = (qb + MOBA_GROUP - 1) // MOBA_GROUP
    final = lax.fori_loop(0, n_groups, per_group, tuple(carry))
    for h in range(hp):
        _, acc = final[h]
        o_ref[:, lanes[h]] = (acc[0:dh] / acc[dh:dh + 1]).T.astype(o_ref.dtype)


def _moba(qkv, batch, seq, n_heads):
    t = qkv.shape[0]
    blk = MOBA_BLOCK
    n_blk = seq // blk
    hp = _tile(n_heads, MOBA_HEADS_PER_STEP)
    assert n_blk % MOBA_GROUP == 0
    kernel = functools.partial(_moba_kernel, n_blk=n_blk, n_heads=n_heads, hp=hp, scale=HEAD_DIM ** -0.5)
    wide = hp * HEAD_DIM
    hsteps = n_heads // hp
    return pl.pallas_call(
        kernel,
        out_shape=jax.ShapeDtypeStruct((t, n_heads * HEAD_DIM), BF16),
        grid=(batch, hsteps, n_blk),
        in_specs=[
            pl.BlockSpec((blk, wide), lambda b, h, i: (b * n_blk + i, h)),
            pl.BlockSpec((seq, wide), lambda b, h, i: (b, hsteps + h), pipeline_mode=pl.Buffered(1)),
            pl.BlockSpec((seq, wide), lambda b, h, i: (b, 2 * hsteps + h), pipeline_mode=pl.Buffered(1)),
        ],
        out_specs=pl.BlockSpec((blk, wide), lambda b, h, i: (b * n_blk + i, h)),
        scratch_shapes=[
            pltpu.VMEM((hp, n_blk, HEAD_DIM), F32),
            pltpu.VMEM((hp, seq, 2 * HEAD_DIM), BF16),
            pltpu.VMEM((hp, HEAD_DIM + MOBA_SUM_ROWS, seq), BF16),
            pltpu.VMEM((hp, n_blk, blk), F32),
            pltpu.VMEM((blk, blk), F32),
        ],
        compiler_params=_params(("parallel", "parallel", "arbitrary")),
        name="moba",
    )(qkv, qkv, qkv)


def _mix_kernel(c_ref, a_ref, gates_ref, x_ref, wpw_ref, wao_ref, wout_ref, g2_ref, wrt_ref, br_ref,
                h_ref, route_ref, cnt_ref, carry_ref):
    i = pl.program_id(0)
    tm, d = x_ref.shape

    @pl.when(i == 0)
    def _():
        carry_ref[...] = jnp.zeros_like(carry_ref)

    yc = jnp.dot(c_ref[...], wpw_ref[...], preferred_element_type=F32)
    ya = jnp.dot(a_ref[...], wao_ref[...], preferred_element_type=F32)
    merged = gates_ref[:, :d].astype(F32) * yc + gates_ref[:, d:].astype(F32) * ya
    h = x_ref[...] + jnp.dot(merged.astype(BF16), wout_ref[...], preferred_element_type=F32)
    h_ref[...] = h
    hn = h * lax.rsqrt(jnp.mean(h * h, axis=-1, keepdims=True) + NORM_EPS) * g2_ref[...]
    logits = lax.dot_general(wrt_ref[...], hn.astype(BF16), NT_DIMS, preferred_element_type=F32) + br_ref[...]

    iota8 = lax.broadcasted_iota(jnp.int32, (N_GROUPS, tm), 0)
    gl = logits[0:N_GROUPS]
    gmax = jnp.max(gl, axis=0, keepdims=True)
    g_p = 1.0 / jnp.sum(jnp.exp(gl - gmax), axis=0, keepdims=True)
    g_idx = jnp.min(jnp.where(gl == gmax, iota8, N_GROUPS), axis=0, keepdims=True)
    esel = jnp.zeros((EXPERTS_PER_GROUP, tm), F32)
    for g in range(N_GROUPS):
        lo = N_GROUPS + g * EXPERTS_PER_GROUP
        esel = jnp.where(g_idx == g, logits[lo:lo + EXPERTS_PER_GROUP], esel)
    v1 = jnp.max(esel, axis=0, keepdims=True)
    i1 = jnp.min(jnp.where(esel == v1, iota8, EXPERTS_PER_GROUP), axis=0, keepdims=True)
    rest = jnp.where(iota8 == i1, -jnp.inf, esel)
    v2 = jnp.max(rest, axis=0, keepdims=True)
    i2 = jnp.min(jnp.where(rest == v2, iota8, EXPERTS_PER_GROUP), axis=0, keepdims=True)
    r = jnp.exp(v2 - v1)
    w1 = g_p / (1.0 + r)
    w2 = g_p * r / (1.0 + r)
    e1 = g_idx * EXPERTS_PER_GROUP + i1
    e2 = g_idx * EXPERTS_PER_GROUP + i2

    iota_e = lax.broadcasted_iota(jnp.int32, (N_EXPERTS, tm), 0)
    oh1 = iota_e == e1
    oh2 = iota_e == e2
    onehot = jnp.concatenate([jnp.where(oh1, 1.0, 0.0), jnp.where(oh2, 1.0, 0.0)], axis=1)
    before = (lax.broadcasted_iota(jnp.int32, (2 * tm, 2 * tm), 0)
              < lax.broadcasted_iota(jnp.int32, (2 * tm, 2 * tm), 1))
    prefix = jnp.dot(onehot.astype(BF16), jnp.where(before, 1.0, 0.0).astype(BF16), preferred_element_type=F32)
    carry = carry_ref[...]
    base = prefix + carry[:, 0:1]
    rank1 = jnp.sum(jnp.where(oh1, base[:, :tm], 0.0), axis=0, keepdims=True)
    rank2 = jnp.sum(jnp.where(oh2, base[:, tm:], 0.0), axis=0, keepdims=True)
    total = carry + jnp.sum(onehot, axis=1, keepdims=True)
    carry_ref[...] = total
    cnt_ref[...] = total

    route_ref[0:1, :] = e1.astype(F32)
    route_ref[1:2, :] = e2.astype(F32)
    route_ref[2:3, :] = w1
    route_ref[3:4, :] = w2
    route_ref[4:5, :] = rank1
    route_ref[5:6, :] = rank2
    route_ref[6:8, :] = jnp.zeros((2, tm), F32)


def _mix(c, a, gates, x, w_pw, w_ao, w_out, g2, wr_t, b_r):
    t, d = x.shape
    cw = c.shape[1]
    aw = a.shape[1]
    tm = _tile(t, 256)
    const = lambda shape: pl.BlockSpec(shape, lambda i: (0, 0), pipeline_mode=pl.Buffered(1))
    return pl.pallas_call(
        _mix_kernel,
        out_shape=(
            jax.ShapeDtypeStruct((t, d), F32),
            jax.ShapeDtypeStruct((8, t), F32),
            jax.ShapeDtypeStruct((N_EXPERTS, 128), F32),
        ),
        grid=(t // tm,),
        in_specs=[
            pl.BlockSpec((tm, cw), lambda i: (i, 0)),
            pl.BlockSpec((tm, aw), lambda i: (i, 0)),
            pl.BlockSpec((tm, 2 * d), lambda i: (i, 0)),
            pl.BlockSpec((tm, d), lambda i: (i, 0)),
            const((cw, d)),
            const((aw, d)),
            const((d, d)),
            const((1, d)),
            const((ROUTER_ROWS, d)),
            const((ROUTER_ROWS, 1)),
        ],
        out_specs=(
            pl.BlockSpec((tm, d), lambda i: (i, 0)),
            pl.BlockSpec((8, tm), lambda i: (0, i)),
            pl.BlockSpec((N_EXPERTS, 128), lambda i: (0, 0)),
        ),
        scratch_shapes=[pltpu.VMEM((N_EXPERTS, 128), F32)],
        compiler_params=_params(("arbitrary",)),
        name="mix",
    )(c, a, gates, x, w_pw, w_ao, w_out, g2.reshape(1, d), wr_t, b_r)


def _row_copy(src_hbm, row, dst_vmem, dst_row, sem):
    return pltpu.make_async_copy(src_hbm.at[pl.ds(row, 1), :], dst_vmem.at[pl.ds(dst_row, 1), :], sem)


def _experts_kernel(blk_expert_ref, slot_tok_ref, n_valid_ref, next_expert_ref, parity_ref,
                    h_hbm, g2_ref, wg_hbm, wu_hbm, wd_hbm, y_ref,
                    xbuf, xsem, wg_f, wu_f, wd_f, wsem, wg_b, wu_b, wd_b):
    n = pl.program_id(0)
    n_valid = n_valid_ref[0]
    rows = MOE_BLOCK
    de = wg_b.shape[1]

    def weight_copies(e, slot):
        return [pltpu.make_async_copy(wg_hbm.at[e], wg_f.at[slot], wsem.at[slot]),
                pltpu.make_async_copy(wu_hbm.at[e], wu_f.at[slot], wsem.at[slot]),
                pltpu.make_async_copy(wd_hbm.at[e], wd_f.at[slot], wsem.at[slot])]

    def row_copy(block, r, slot):
        return _row_copy(h_hbm, slot_tok_ref[block * rows + r], xbuf.at[slot], r, xsem.at[slot])

    @pl.when(n == 0)
    def _():
        for cp in weight_copies(blk_expert_ref[0], 0):
            cp.start()
        for r in range(rows):
            row_copy(0, r, 0).start()

    @pl.when(n >= n_valid)
    def _():
        y_ref[...] = jnp.zeros_like(y_ref)

    @pl.when(n < n_valid)
    def _():
        e = blk_expert_ref[n]
        wslot = parity_ref[e]

        @pl.when(jnp.logical_or(n == 0, blk_expert_ref[jnp.maximum(n - 1, 0)] != e))
        def _():
            nxt = next_expert_ref[e]

            @pl.when(nxt < N_EXPERTS)
            def _():
                for cp in weight_copies(nxt, 1 - wslot):
                    cp.start()

            for cp in weight_copies(e, wslot):
                cp.wait()
            wg_b[...] = wg_f[wslot].astype(BF16)
            wu_b[...] = wu_f[wslot].astype(BF16)
            wd_b[...] = wd_f[wslot].astype(BF16)

        slot = n % 2
        for r in range(rows):
            row_copy(n, r, slot).wait()
        nb = jnp.minimum(n + 1, n_valid - 1)
        col = min(256, de)
        pieces = 2 * (de // col) + y_ref.shape[1] // col
        per_piece = -(-rows // pieces)
        issued = [0]

        def issue_some():
            lo = issued[0]
            hi = min(rows, lo + per_piece)
            for r in range(lo, hi):
                row_copy(nb, r, 1 - slot).start()
            issued[0] = hi

        x = xbuf[slot]
        xn = x * lax.rsqrt(jnp.mean(x * x, axis=-1, keepdims=True) + NORM_EPS) * g2_ref[...]
        xb = xn.astype(BF16)
        hidden = []
        for c in range(de // col):
            cs = slice(c * col, (c + 1) * col)
            gate = jnp.dot(xb, wg_b[:, cs], preferred_element_type=F32)
            issue_some()
            up = jnp.dot(xb, wu_b[:, cs], preferred_element_type=F32)
            issue_some()
            hidden.append((gate * jax.nn.sigmoid(gate) * up).astype(BF16))
        hidden = jnp.concatenate(hidden, axis=1)
        for c in range(y_ref.shape[1] // col):
            cs = slice(c * col, (c + 1) * col)
            y_ref[:, cs] = jnp.dot(hidden, wd_b[:, cs], preferred_element_type=F32)
            issue_some()
        assert issued[0] == rows

        @pl.when(n == n_valid - 1)
        def _():
            for r in range(rows):
                row_copy(nb, r, 1 - slot).wait()


def _experts(h, g2, w_g, w_u, w_d, blk_expert, slot_tok, n_valid, next_expert, parity):
    t, d = h.shape
    de = w_g.shape[2]
    n_blocks = blk_expert.shape[0]
    rows = MOE_BLOCK
    grid_spec = pltpu.PrefetchScalarGridSpec(
        num_scalar_prefetch=5,
        grid=(n_blocks,),
        in_specs=[
            pl.BlockSpec(memory_space=pl.ANY),
            pl.BlockSpec((1, d), lambda n, *_: (0, 0)),
            pl.BlockSpec(memory_space=pl.ANY),
            pl.BlockSpec(memory_space=pl.ANY),
            pl.BlockSpec(memory_space=pl.ANY),
        ],
        out_specs=pl.BlockSpec((rows, d), lambda n, *_: (n, 0)),
        scratch_shapes=[
            pltpu.VMEM((2, rows, d), F32),
            pltpu.SemaphoreType.DMA((2,)),
            pltpu.VMEM((2, d, de), F32),
            pltpu.VMEM((2, d, de), F32),
            pltpu.VMEM((2, de, d), F32),
            pltpu.SemaphoreType.DMA((2,)),
            pltpu.VMEM((d, de), BF16),
            pltpu.VMEM((d, de), BF16),
            pltpu.VMEM((de, d), BF16),
        ],
    )
    return pl.pallas_call(
        _experts_kernel,
        out_shape=jax.ShapeDtypeStruct((n_blocks * rows, d), F32),
        grid_spec=grid_spec,
        compiler_params=_params(("arbitrary",)),
        name="experts",
    )(blk_expert, slot_tok, n_valid, next_expert, parity, h, g2.reshape(1, d), w_g, w_u, w_d)


def _combine_kernel(slot1_ref, slot2_ref, h_ref, w_ref, gf_ref, y_hbm, o_ref, ybuf, sems):
    i = pl.program_id(0)
    n_tiles = pl.num_programs(0)
    tm = h_ref.shape[0]

    def gather(tile, buf):
        cps = []
        for r in range(tm):
            cps.append(_row_copy(y_hbm, slot1_ref[tile * tm + r], ybuf.at[buf, 0], r, sems.at[buf]))
            cps.append(_row_copy(y_hbm, slot2_ref[tile * tm + r], ybuf.at[buf, 1], r, sems.at[buf]))
        return cps

    @pl.when(i == 0)
    def _():
        for cp in gather(0, 0):
            cp.start()

    @pl.when(i + 1 < n_tiles)
    def _():
        for cp in gather(i + 1, (i + 1) % 2):
            cp.start()

    buf = i % 2
    for cp in gather(i, buf):
        cp.wait()
    w = w_ref[...]
    h = h_ref[...] + w[:, 0:1] * ybuf[buf, 0] + w[:, 1:2] * ybuf[buf, 1]
    o_ref[...] = h * lax.rsqrt(jnp.mean(h * h, axis=-1, keepdims=True) + NORM_EPS) * gf_ref[...]


def _combine(h, y, slot1, slot2, w, gf):
    t, d = h.shape
    tm = _tile(t, 128)
    grid_spec = pltpu.PrefetchScalarGridSpec(
        num_scalar_prefetch=2,
        grid=(t // tm,),
        in_specs=[
            pl.BlockSpec((tm, d), lambda i, s1, s2: (i, 0)),
            pl.BlockSpec((tm, 2), lambda i, s1, s2: (i, 0)),
            pl.BlockSpec((1, d), lambda i, s1, s2: (0, 0)),
            pl.BlockSpec(memory_space=pl.ANY),
        ],
        out_specs=pl.BlockSpec((tm, d), lambda i, s1, s2: (i, 0)),
        scratch_shapes=[pltpu.VMEM((2, 2, tm, d), F32), pltpu.SemaphoreType.DMA((2,))],
    )
    return pl.pallas_call(
        _combine_kernel,
        out_shape=jax.ShapeDtypeStruct((t, d), F32),
        grid_spec=grid_spec,
        compiler_params=_params(("arbitrary",)),
        name="combine",
    )(slot1, slot2, h, w, gf.reshape(1, d), y)


def _layer(h, batch, seq, norm1_g, w_in, conv_dw_w, conv_dw_b, conv_ln_g, conv_ln_b, w_conv_out, w_attn_out,
           gate_b, w_out, norm2_g, w_rg, b_rg, w_re, b_re, w_exp_gate, w_exp_up, w_exp_down):
    t, d = h.shape
    c = w_conv_out.shape[0]
    aw = w_attn_out.shape[0]
    n_heads = aw // HEAD_DIM
    c1 = 2 * c
    c2 = c1 + 3 * aw

    u = _rmsnorm(h, norm1_g, BF16)
    w_in_b = w_in.astype(BF16)
    z = _proj_call(_glu_proj_kernel, u, [w_in_b[:, :c], w_in_b[:, c:c1]], [], c, F32, "glu_proj")
    qkv = _proj_call(_proj_kernel, u, [w_in_b[:, c1:c2]], [], 3 * aw, BF16, "qkv_proj")
    gates = _proj_call(_gate_proj_kernel, u, [w_in_b[:, c2:]], [gate_b.reshape(1, 2 * d)], 2 * d, BF16, "gate_proj")

    conv = _conv_branch(z, seq, conv_dw_w, conv_dw_b, conv_ln_g, conv_ln_b)
    attn = _moba(qkv, batch, seq, n_heads)

    wr_t = jnp.zeros((ROUTER_ROWS, d), F32).at[:N_GROUPS].set(w_rg.T).at[N_GROUPS:N_GROUPS + N_EXPERTS].set(w_re.T)
    b_r = jnp.zeros((ROUTER_ROWS,), F32).at[:N_GROUPS].set(b_rg).at[N_GROUPS:N_GROUPS + N_EXPERTS].set(b_re)
    h_mid, route, counts = _mix(conv, attn, gates, h, w_conv_out.astype(BF16), w_attn_out.astype(BF16),
                                w_out.astype(BF16), norm2_g, wr_t.astype(BF16), b_r.reshape(ROUTER_ROWS, 1))

    n_assign = 2 * t
    n_blocks = -(-n_assign // MOE_BLOCK) + N_EXPERTS
    counts = counts[:, 0].astype(jnp.int32)
    padded = (counts + MOE_BLOCK - 1) // MOE_BLOCK * MOE_BLOCK
    pad_end = jnp.cumsum(padded)
    pad_start = pad_end - padded
    e1 = route[0].astype(jnp.int32)
    e2 = route[1].astype(jnp.int32)
    slot1 = pad_start[e1] + route[4].astype(jnp.int32)
    slot2 = pad_start[e2] + route[5].astype(jnp.int32)
    tok = jnp.arange(t, dtype=jnp.int32)
    slot_tok = jnp.zeros((n_blocks * MOE_BLOCK,), jnp.int32).at[jnp.concatenate([slot1, slot2])].set(
        jnp.concatenate([tok, tok]))
    blk_start = jnp.arange(n_blocks, dtype=jnp.int32) * MOE_BLOCK
    blk_expert = jnp.minimum(jnp.sum(pad_end[None, :] <= blk_start[:, None], axis=1), N_EXPERTS - 1).astype(jnp.int32)
    n_valid = (pad_end[-1:] // MOE_BLOCK).astype(jnp.int32)
    ids = jnp.arange(N_EXPERTS, dtype=jnp.int32)
    nonempty = counts > 0
    later = jnp.logical_and(ids[None, :] > ids[:, None], nonempty[None, :])
    next_expert = jnp.min(jnp.where(later, ids[None, :], N_EXPERTS), axis=1).astype(jnp.int32)
    parity = (jnp.sum(jnp.logical_and(ids[None, :] < ids[:, None], nonempty[None, :]), axis=1) % 2).astype(jnp.int32)

    y = _experts(h_mid, norm2_g, w_exp_gate, w_exp_up, w_exp_down, blk_expert, slot_tok, n_valid,
                 next_expert, parity)
    weights = jnp.stack([route[2], route[3]], axis=1)
    return h_mid, y, slot1, slot2, weights


def kernel(x, norm1_g, w_in, conv_dw_w, conv_dw_b, conv_ln_g, conv_ln_b, w_conv_out, w_attn_out, gate_b, w_out,
           norm2_g, w_router_group, b_router_group, w_router_expert, b_router_expert, w_exp_gate, w_exp_up,
           w_exp_down, norm_f_g):
    b, s, d = x.shape
    assert norm1_g.shape[0] == 1, "single-layer stack: the combine kernel fuses the final RMSNorm"
    h_mid, y, slot1, slot2, weights = _layer(
        x.reshape(b * s, d), b, s, norm1_g[0], w_in[0], conv_dw_w[0], conv_dw_b[0], conv_ln_g[0], conv_ln_b[0],
        w_conv_out[0], w_attn_out[0], gate_b[0], w_out[0], norm2_g[0], w_router_group[0],
        b_router_group[0], w_router_expert[0], b_router_expert[0], w_exp_gate[0], w_exp_up[0], w_exp_down[0])
    return _combine(h_mid, y, slot1, slot2, weights, norm_f_g).reshape(b, s, d)
```

```python
import functools

import jax
import jax.numpy as jnp
from jax import lax
from jax.experimental import pallas as pl
from jax.experimental.pallas import tpu as pltpu

F32 = jnp.float32
BF16 = jnp.bfloat16

NORM_EPS = 1e-6
NEG_BIG = -1e30
MASKED = -2e30
LOG2_E = 1.4426950408889634
MOBA_GROUP = 4
MOBA_HEADS_PER_STEP = 4
MOBA_BIAS_PIECES = 3
MOBA_SUM_ROWS = 8
CONV_TAPS = 31
HEAD_DIM = 128
MOBA_BLOCK = 256
MOBA_TOPK = 3
N_GROUPS = 8
EXPERTS_PER_GROUP = 8
N_EXPERTS = N_GROUPS * EXPERTS_PER_GROUP
MOE_BLOCK = 256
ROW_RING = 3
ROUTER_ROWS = 128
CONV_HALO = 32
SUBLANES = 8
V7X_VMEM_LIMIT_BYTES = 56 * 1024 * 1024

NT_DIMS = (((1,), (1,)), ((), ()))


def _params(semantics, flags=None):
    return pltpu.CompilerParams(dimension_semantics=semantics, vmem_limit_bytes=V7X_VMEM_LIMIT_BYTES, flags=flags)


def _tile(n, want):
    t = min(n, want)
    while n % t:
        t //= 2
    return t


def _rmsnorm_kernel(x_ref, g_ref, o_ref):
    x = x_ref[...]
    ms = jnp.mean(x * x, axis=-1, keepdims=True)
    o_ref[...] = (x * lax.rsqrt(ms + NORM_EPS) * g_ref[...]).astype(o_ref.dtype)


def _rmsnorm(x, g, out_dtype):
    t, d = x.shape
    tm = _tile(t, 512)
    return pl.pallas_call(
        _rmsnorm_kernel,
        out_shape=jax.ShapeDtypeStruct((t, d), out_dtype),
        grid=(t // tm,),
        in_specs=[pl.BlockSpec((tm, d), lambda i: (i, 0)), pl.BlockSpec((1, d), lambda i: (0, 0))],
        out_specs=pl.BlockSpec((tm, d), lambda i: (i, 0)),
        compiler_params=_params(("parallel",)),
        name="rmsnorm",
    )(x, g.reshape(1, d))


def _glu_proj_kernel(u_ref, wv_ref, wg_ref, o_ref):
    u = u_ref[...]
    val = jnp.dot(u, wv_ref[...], preferred_element_type=F32)
    gate = jnp.dot(u, wg_ref[...], preferred_element_type=F32)
    o_ref[...] = val * jax.nn.sigmoid(gate)


def _proj_kernel(u_ref, w_ref, o_ref):
    o_ref[...] = jnp.dot(u_ref[...], w_ref[...], preferred_element_type=F32).astype(o_ref.dtype)


def _gate_proj_kernel(u_ref, w_ref, b_ref, o_ref):
    y = jnp.dot(u_ref[...], w_ref[...], preferred_element_type=F32) + b_ref[...]
    o_ref[...] = jax.nn.sigmoid(y).astype(o_ref.dtype)


def _proj_call(kernel, u, weights, extra, n_out, out_dtype, name):
    t, k = u.shape
    tm = _tile(t, 1024)
    tn = _tile(n_out, 512)
    in_specs = [pl.BlockSpec((tm, k), lambda i, j: (i, 0))]
    in_specs += [pl.BlockSpec((k, tn), lambda i, j: (0, j)) for _ in weights]
    in_specs += [pl.BlockSpec((1, tn), lambda i, j: (0, j)) for _ in extra]
    return pl.pallas_call(
        kernel,
        out_shape=jax.ShapeDtypeStruct((t, n_out), out_dtype),
        grid=(t // tm, n_out // tn),
        in_specs=in_specs,
        out_specs=pl.BlockSpec((tm, tn), lambda i, j: (i, j)),
        compiler_params=_params(("parallel", "parallel")),
        name=name,
    )(u, *weights, *extra)


def _conv_kernel(z_ref, halo_ref, w_ref, b_ref, g_ref, beta_ref, o_ref, ext_ref, acc_ref, *, tiles_per_seq):
    i = pl.program_id(0)
    ts, c = z_ref.shape
    seq_start = (i % tiles_per_seq) == 0
    n_lt = c // 128
    span = ts + CONV_HALO - SUBLANES
    for lt in range(n_lt):
        ls = slice(lt * 128, (lt + 1) * 128)
        ext_ref[0, lt, 0:CONV_HALO, :] = jnp.where(seq_start, 0.0, halo_ref[:, ls])
        ext_ref[0, lt, CONV_HALO:, :] = z_ref[:, ls]
        for s in range(1, SUBLANES):
            ext_ref[s, lt, 0:span, :] = ext_ref[0, lt, s:s + span, :]
    rows = 64
    first = CONV_HALO - (CONV_TAPS - 1)
    n_rc = ts // rows

    def chunk(ci, carry):
        lt = ci // n_rc
        r0 = pl.multiple_of((ci % n_rc) * rows, rows)
        accs = [jnp.broadcast_to(b_ref[lt], (rows, 128)), jnp.zeros((rows, 128), F32)]
        for tap in range(CONV_TAPS):
            s = (first + tap) % SUBLANES
            base = pl.multiple_of(r0 + (first + tap - s), SUBLANES)
            accs[tap % 2] = accs[tap % 2] + ext_ref[s, lt, pl.ds(base, rows), :] * w_ref[lt, tap:tap + 1, :]
        acc_ref[lt, pl.ds(r0, rows), :] = accs[0] + accs[1]
        return carry

    lax.fori_loop(0, n_lt * n_rc, chunk, 0)
    y = jnp.concatenate([acc_ref[lt] for lt in range(n_lt)], axis=1)
    mu = jnp.mean(y, axis=-1, keepdims=True)
    yc = y - mu
    var = jnp.mean(yc * yc, axis=-1, keepdims=True)
    yn = yc * lax.rsqrt(var + NORM_EPS) * g_ref[...] + beta_ref[...]
    o_ref[...] = (yn * jax.nn.sigmoid(yn)).astype(o_ref.dtype)


def _conv_branch(z, seq, w_dw, b_dw, ln_g, ln_b):
    t, c = z.shape
    ts = _tile(seq, 256)
    hb = ts // CONV_HALO
    kernel = functools.partial(_conv_kernel, tiles_per_seq=seq // ts)
    row = lambda v: v.reshape(1, c)
    n_lt = c // 128
    w_lt = w_dw.reshape(CONV_TAPS, n_lt, 128).transpose(1, 0, 2)
    b_lt = b_dw.reshape(n_lt, 1, 128)
    return pl.pallas_call(
        kernel,
        out_shape=jax.ShapeDtypeStruct((t, c), BF16),
        grid=(t // ts,),
        in_specs=[
            pl.BlockSpec((ts, c), lambda i: (i, 0)),
            pl.BlockSpec((CONV_HALO, c), lambda i: (jnp.maximum(i * hb - 1, 0), 0)),
            pl.BlockSpec((n_lt, CONV_TAPS, 128), lambda i: (0, 0, 0)),
            pl.BlockSpec((n_lt, 1, 128), lambda i: (0, 0, 0)),
            pl.BlockSpec((1, c), lambda i: (0, 0)),
            pl.BlockSpec((1, c), lambda i: (0, 0)),
        ],
        out_specs=pl.BlockSpec((ts, c), lambda i: (i, 0)),
        scratch_shapes=[pltpu.VMEM((SUBLANES, n_lt, ts + CONV_HALO, 128), F32), pltpu.VMEM((n_lt, ts, 128), F32)],
        compiler_params=_params(("parallel",)),
        name="conv",
    )(z, z, w_lt, b_lt, row(ln_g), row(ln_b))


def _moba_kernel(q_ref, k_ref, v_ref, o_ref, kmean_ref, kaug_ref, vt_ref, sel_ref, mask_ref,
                 *, n_blk, n_heads, hp, scale):
    head0 = pl.program_id(1) * hp
    qb = pl.program_id(2)
    blk = MOBA_BLOCK
    dh = HEAD_DIM
    lanes = [slice(h * dh, (h + 1) * dh) for h in range(hp)]
    slope2 = [jnp.exp2(-8.0 * (head0 + h + 1).astype(F32) / n_heads) * LOG2_E for h in range(hp)]

    @pl.when(qb == 0)
    def _():
        k_local = lax.broadcasted_iota(jnp.int32, (blk, dh), 0).astype(F32)
        col = lax.broadcasted_iota(jnp.int32, (blk, dh), 1)
        extras = []
        for h in range(hp):
            rest = slope2[h] * k_local
            extra = jnp.zeros((blk, dh), F32)
            for piece in range(MOBA_BIAS_PIECES):
                part = rest.astype(BF16).astype(F32)
                extra = jnp.where(col == piece, part, extra)
                rest = rest - part
            extras.append(extra.astype(BF16))

        def per_block(kb, carry):
            st = pl.multiple_of(kb * blk, blk)
            for h in range(hp):
                k = k_ref[pl.ds(st, blk), lanes[h]]
                kmean_ref[h, pl.ds(kb, 1), :] = jnp.mean(k.astype(F32), axis=0, keepdims=True)
                kaug_ref[h, pl.ds(st, blk), 0:dh] = k
                kaug_ref[h, pl.ds(st, blk), dh:2 * dh] = extras[h]
                vt_ref[h, 0:dh, pl.ds(st, blk)] = v_ref[pl.ds(st, blk), lanes[h]].astype(F32).T.astype(BF16)
            return carry
        lax.fori_loop(0, n_blk, per_block, 0)
        ones_row = lax.broadcasted_iota(jnp.int32, (MOBA_SUM_ROWS, vt_ref.shape[2]), 0) == 0
        for h in range(hp):
            vt_ref[h, dh:dh + MOBA_SUM_ROWS, :] = jnp.where(ones_row, 1.0, 0.0).astype(BF16)
        key_i = lax.broadcasted_iota(jnp.int32, (blk, blk), 0)
        query_i = lax.broadcasted_iota(jnp.int32, (blk, blk), 1)
        mask_ref[...] = jnp.where(query_i >= key_i, 0.0, MASKED)

    rows = lax.broadcasted_iota(jnp.int32, (n_blk, blk), 0)
    ones_cols = jnp.where(lax.broadcasted_iota(jnp.int32, (blk, dh), 1) < MOBA_BIAS_PIECES, 1.0, 0.0).astype(BF16)
    q_aug = []
    for h in range(hp):
        q = q_ref[:, lanes[h]]
        gate = lax.dot_general(kmean_ref[h].astype(BF16), q, NT_DIMS, preferred_element_type=F32)
        gate = jnp.where(rows < qb, gate, -jnp.inf)
        sel = jnp.zeros((n_blk, blk), F32)
        for _ in range(MOBA_TOPK):
            top = jnp.max(gate, axis=0, keepdims=True)
            idx = jnp.min(jnp.where(gate == top, rows, n_blk), axis=0, keepdims=True)
            pick = rows == idx
            sel = jnp.where(pick, jnp.where(top > -jnp.inf, 1.0, sel), sel)
            gate = jnp.where(pick, -jnp.inf, gate)
        sel_ref[h] = sel
        qs = (q.astype(F32) * (scale * LOG2_E)).astype(BF16)
        q_aug.append(jnp.concatenate([qs, ones_cols], axis=1))

    def block_scores(h, st):
        return lax.dot_general(kaug_ref[h, pl.ds(st, blk), :], q_aug[h], NT_DIMS, preferred_element_type=F32)

    own_st = pl.multiple_of(qb * blk, blk)
    own = [block_scores(h, own_st) + mask_ref[...] for h in range(hp)]
    carry = []
    for h in range(hp):
        m = jnp.max(own[h], axis=0, keepdims=True)
        p = jnp.exp2(own[h] - m).astype(BF16)
        carry.append((m, jnp.dot(vt_ref[h, :, pl.ds(own_st, blk)], p, preferred_element_type=F32)))

    def per_group(j, carry):
        starts = [pl.multiple_of((j * MOBA_GROUP + g) * blk, blk) for g in range(MOBA_GROUP)]
        chains = [(h, j * MOBA_GROUP + g, starts[g]) for h in range(hp) for g in range(MOBA_GROUP)]
        scores = [block_scores(h, st) for h, _, st in chains]
        stats = []
        for (h, kb, _), s in zip(chains, scores):
            mb = jnp.max(s, axis=0, keepdims=True)
            p = jnp.exp2(s - mb).astype(BF16)
            block_term = slope2[h] * ((kb - qb) * blk).astype(F32)
            mb = jnp.where(sel_ref[h, pl.ds(kb, 1), :] > 0.5, mb + block_term, MASKED)
            stats.append((mb, p))
        pvs = [jnp.dot(vt_ref[h, :, pl.ds(st, blk)], p, preferred_element_type=F32)
               for (h, _, st), (_, p) in zip(chains, stats)]
        out = []
        for h in range(hp):
            m, acc = carry[h]
            mine = [(mb, pv) for (hh, _, _), (mb, _), pv in zip(chains, stats, pvs) if hh == h]
            m_new = m
            for mb, _ in mine:
                m_new = jnp.maximum(m_new, mb)
            acc = jnp.exp2(m - m_new) * acc
            for mb, pv in mine:
                acc = acc + jnp.exp2(mb - m_new) * pv
            out.append((m_new, acc))
        return tuple(out)

    n_groups = (qb + MOBA_GROUP - 1) // MOBA_GROUP
    final = lax.fori_loop(0, n_groups, per_group, tuple(carry))
    for h in range(hp):
        _, acc = final[h]
        o_ref[:, lanes[h]] = (acc[0:dh] / acc[dh:dh + 1]).T.astype(o_ref.dtype)


def _moba(qkv, batch, seq, n_heads):
    t = qkv.shape[0]
    blk = MOBA_BLOCK
    n_blk = seq // blk
    hp = _tile(n_heads, MOBA_HEADS_PER_STEP)
    assert n_blk % MOBA_GROUP == 0
    kernel = functools.partial(_moba_kernel, n_blk=n_blk, n_heads=n_heads, hp=hp, scale=HEAD_DIM ** -0.5)
    wide = hp * HEAD_DIM
    hsteps = n_heads // hp
    return pl.pallas_call(
        kernel,
        out_shape=jax.ShapeDtypeStruct((t, n_heads * HEAD_DIM), BF16),
        grid=(batch, hsteps, n_blk),
        in_specs=[
            pl.BlockSpec((blk, wide), lambda b, h, i: (b * n_blk + i, h)),
            pl.BlockSpec((seq, wide), lambda b, h, i: (b, hsteps + h), pipeline_mode=pl.Buffered(1)),
            pl.BlockSpec((seq, wide), lambda b, h, i: (b, 2 * hsteps + h), pipeline_mode=pl.Buffered(1)),
        ],
        out_specs=pl.BlockSpec((blk, wide), lambda b, h, i: (b * n_blk + i, h)),
        scratch_shapes=[
            pltpu.VMEM((hp, n_blk, HEAD_DIM), F32),
            pltpu.VMEM((hp, seq, 2 * HEAD_DIM), BF16),
            pltpu.VMEM((hp, HEAD_DIM + MOBA_SUM_ROWS, seq), BF16),
            pltpu.VMEM((hp, n_blk, blk), F32),
            pltpu.VMEM((blk, blk), F32),
        ],
        compiler_params=_params(("parallel", "parallel", "arbitrary")),
        name="moba",
    )(qkv, qkv, qkv)


def _mix_kernel(c_ref, a_ref, gates_ref, x_ref, wpw_ref, wao_ref, wout_ref, g2_ref, wrt_ref, br_ref,
                h_ref, route_ref, cnt_ref, carry_ref):
    i = pl.program_id(0)
    tm, d = x_ref.shape

    @pl.when(i == 0)
    def _():
        carry_ref[...] = jnp.zeros_like(carry_ref)

    yc = jnp.dot(c_ref[...], wpw_ref[...], preferred_element_type=F32)
    ya = jnp.dot(a_ref[...], wao_ref[...], preferred_element_type=F32)
    merged = gates_ref[:, :d].astype(F32) * yc + gates_ref[:, d:].astype(F32) * ya
    h = x_ref[...] + jnp.dot(merged.astype(BF16), wout_ref[...], preferred_element_type=F32)
    h_ref[...] = h
    hn = h * lax.rsqrt(jnp.mean(h * h, axis=-1, keepdims=True) + NORM_EPS) * g2_ref[...]
    logits = lax.dot_general(wrt_ref[...], hn.astype(BF16), NT_DIMS, preferred_element_type=F32) + br_ref[...]

    iota8 = lax.broadcasted_iota(jnp.int32, (N_GROUPS, tm), 0)
    gl = logits[0:N_GROUPS]
    gmax = jnp.max(gl, axis=0, keepdims=True)
    g_p = 1.0 / jnp.sum(jnp.exp(gl - gmax), axis=0, keepdims=True)
    g_idx = jnp.min(jnp.where(gl == gmax, iota8, N_GROUPS), axis=0, keepdims=True)
    esel = jnp.zeros((EXPERTS_PER_GROUP, tm), F32)
    for g in range(N_GROUPS):
        lo = N_GROUPS + g * EXPERTS_PER_GROUP
        esel = jnp.where(g_idx == g, logits[lo:lo + EXPERTS_PER_GROUP], esel)
    v1 = jnp.max(esel, axis=0, keepdims=True)
    i1 = jnp.min(jnp.where(esel == v1, iota8, EXPERTS_PER_GROUP), axis=0, keepdims=True)
    rest = jnp.where(iota8 == i1, -jnp.inf, esel)
    v2 = jnp.max(rest, axis=0, keepdims=True)
    i2 = jnp.min(jnp.where(rest == v2, iota8, EXPERTS_PER_GROUP), axis=0, keepdims=True)
    r = jnp.exp(v2 - v1)
    w1 = g_p / (1.0 + r)
    w2 = g_p * r / (1.0 + r)
    e1 = g_idx * EXPERTS_PER_GROUP + i1
    e2 = g_idx * EXPERTS_PER_GROUP + i2

    iota_e = lax.broadcasted_iota(jnp.int32, (N_EXPERTS, tm), 0)
    oh1 = iota_e == e1
    oh2 = iota_e == e2
    onehot = jnp.concatenate([jnp.where(oh1, 1.0, 0.0), jnp.where(oh2, 1.0, 0.0)], axis=1)
    before = (lax.broadcasted_iota(jnp.int32, (2 * tm, 2 * tm), 0)
              < lax.broadcasted_iota(jnp.int32, (2 * tm, 2 * tm), 1))
    prefix = jnp.dot(onehot.astype(BF16), jnp.where(before, 1.0, 0.0).astype(BF16), preferred_element_type=F32)
    carry = carry_ref[...]
    base = prefix + carry[:, 0:1]
    rank1 = jnp.sum(jnp.where(oh1, base[:, :tm], 0.0), axis=0, keepdims=True)
    rank2 = jnp.sum(jnp.where(oh2, base[:, tm:], 0.0), axis=0, keepdims=True)
    total = carry + jnp.sum(onehot, axis=1, keepdims=True)
    carry_ref[...] = total
    cnt_ref[...] = total

    route_ref[0:1, :] = e1.astype(F32)
    route_ref[1:2, :] = e2.astype(F32)
    route_ref[2:3, :] = w1
    route_ref[3:4, :] = w2
    route_ref[4:5, :] = rank1
    route_ref[5:6, :] = rank2
    route_ref[6:8, :] = jnp.zeros((2, tm), F32)


def _mix(c, a, gates, x, w_pw, w_ao, w_out, g2, wr_t, b_r):
    t, d = x.shape
    cw = c.shape[1]
    aw = a.shape[1]
    tm = _tile(t, 256)
    const = lambda shape: pl.BlockSpec(shape, lambda i: (0, 0), pipeline_mode=pl.Buffered(1))
    return pl.pallas_call(
        _mix_kernel,
        out_shape=(
            jax.ShapeDtypeStruct((t, d), F32),
            jax.ShapeDtypeStruct((8, t), F32),
            jax.ShapeDtypeStruct((N_EXPERTS, 128), F32),
        ),
        grid=(t // tm,),
        in_specs=[
            pl.BlockSpec((tm, cw), lambda i: (i, 0)),
            pl.BlockSpec((tm, aw), lambda i: (i, 0)),
            pl.BlockSpec((tm, 2 * d), lambda i: (i, 0)),
            pl.BlockSpec((tm, d), lambda i: (i, 0)),
            const((cw, d)),
            const((aw, d)),
            const((d, d)),
            const((1, d)),
            const((ROUTER_ROWS, d)),
            const((ROUTER_ROWS, 1)),
        ],
        out_specs=(
            pl.BlockSpec((tm, d), lambda i: (i, 0)),
            pl.BlockSpec((8, tm), lambda i: (0, i)),
            pl.BlockSpec((N_EXPERTS, 128), lambda i: (0, 0)),
        ),
        scratch_shapes=[pltpu.VMEM((N_EXPERTS, 128), F32)],
        compiler_params=_params(("arbitrary",)),
        name="mix",
    )(c, a, gates, x, w_pw, w_ao, w_out, g2.reshape(1, d), wr_t, b_r)


def _row_copy(src_hbm, row, dst_vmem, dst_row, sem):
    return pltpu.make_async_copy(src_hbm.at[pl.ds(row, 1), :], dst_vmem.at[pl.ds(dst_row, 1), :], sem)


def _experts_kernel(blk_expert_ref, slot_tok_ref, n_valid_ref, next_expert_ref, parity_ref,
                    h_hbm, g2_ref, wg_hbm, wu_hbm, wd_hbm, y_ref,
                    xbuf, xsem, wg_f, wu_f, wd_f, wsem, wg_b, wu_b, wd_b):
    n = pl.program_id(0)
    n_valid = n_valid_ref[0]
    rows = MOE_BLOCK
    de = wg_b.shape[1]

    def weight_copies(e, slot):
        return [pltpu.make_async_copy(wg_hbm.at[e], wg_f.at[slot], wsem.at[slot]),
                pltpu.make_async_copy(wu_hbm.at[e], wu_f.at[slot], wsem.at[slot]),
                pltpu.make_async_copy(wd_hbm.at[e], wd_f.at[slot], wsem.at[slot])]

    def row_copy(block, r, slot):
        return _row_copy(h_hbm, slot_tok_ref[block * rows + r], xbuf.at[slot], r, xsem.at[slot])

    def start_weights(e, slot):
        for cp in weight_copies(e, slot):
            cp.start(priority=1)

    @pl.when(n == 0)
    def _():
        start_weights(blk_expert_ref[0], 0)
        for r in range(rows):
            row_copy(0, r, 0).start()
        for r in range(rows):
            row_copy(jnp.minimum(1, n_valid - 1), r, 1).start()

    @pl.when(n >= n_valid)
    def _():
        y_ref[...] = jnp.zeros_like(y_ref)

    @pl.when(n < n_valid)
    def _():
        e = blk_expert_ref[n]
        wslot = parity_ref[e]

        @pl.when(jnp.logical_or(n == 0, blk_expert_ref[jnp.maximum(n - 1, 0)] != e))
        def _():
            nxt = next_expert_ref[e]

            @pl.when(nxt < N_EXPERTS)
            def _():
                start_weights(nxt, 1 - wslot)

            for cp in weight_copies(e, wslot):
                cp.wait()
            wg_b[...] = wg_f[wslot].astype(BF16)
            wu_b[...] = wu_f[wslot].astype(BF16)
            wd_b[...] = wd_f[wslot].astype(BF16)

        slot = n % ROW_RING
        for r in range(rows):
            row_copy(n, r, slot).wait()
        nb = jnp.minimum(n + 2, n_valid - 1)
        nslot = (n + 2) % ROW_RING
        col = min(256, de)
        pieces = 2 * (de // col) + y_ref.shape[1] // col
        per_piece = -(-rows // pieces)
        issued = [0]

        def issue_some():
            lo = issued[0]
            hi = min(rows, lo + per_piece)
            for r in range(lo, hi):
                row_copy(nb, r, nslot).start()
            issued[0] = hi

        x = xbuf[slot]
        xn = x * lax.rsqrt(jnp.mean(x * x, axis=-1, keepdims=True) + NORM_EPS) * g2_ref[...]
        xb = xn.astype(BF16)
        hidden = []
        for c in range(de // col):
            cs = slice(c * col, (c + 1) * col)
            gate = jnp.dot(xb, wg_b[:, cs], preferred_element_type=F32)
            issue_some()
            up = jnp.dot(xb, wu_b[:, cs], preferred_element_type=F32)
            issue_some()
            hidden.append((gate * jax.nn.sigmoid(gate) * up).astype(BF16))
        hidden = jnp.concatenate(hidden, axis=1)
        for c in range(y_ref.shape[1] // col):
            cs = slice(c * col, (c + 1) * col)
            y_ref[:, cs] = jnp.dot(hidden, wd_b[:, cs], preferred_element_type=F32)
            issue_some()
        assert issued[0] == rows

        @pl.when(n == n_valid - 1)
        def _():
            for ahead in (1, 2):
                for r in range(rows):
                    row_copy(n, r, (n + ahead) % ROW_RING).wait()


def _experts(h, g2, w_g, w_u, w_d, blk_expert, slot_tok, n_valid, next_expert, parity):
    t, d = h.shape
    de = w_g.shape[2]
    n_blocks = blk_expert.shape[0]
    rows = MOE_BLOCK
    grid_spec = pltpu.PrefetchScalarGridSpec(
        num_scalar_prefetch=5,
        grid=(n_blocks,),
        in_specs=[
            pl.BlockSpec(memory_space=pl.ANY),
            pl.BlockSpec((1, d), lambda n, *_: (0, 0)),
            pl.BlockSpec(memory_space=pl.ANY),
            pl.BlockSpec(memory_space=pl.ANY),
            pl.BlockSpec(memory_space=pl.ANY),
        ],
        out_specs=pl.BlockSpec((rows, d), lambda n, *_: (n, 0)),
        scratch_shapes=[
            pltpu.VMEM((ROW_RING, rows, d), F32),
            pltpu.SemaphoreType.DMA((ROW_RING,)),
            pltpu.VMEM((2, d, de), F32),
            pltpu.VMEM((2, d, de), F32),
            pltpu.VMEM((2, de, d), F32),
            pltpu.SemaphoreType.DMA((2,)),
            pltpu.VMEM((d, de), BF16),
            pltpu.VMEM((d, de), BF16),
            pltpu.VMEM((de, d), BF16),
        ],
    )
    return pl.pallas_call(
        _experts_kernel,
        out_shape=jax.ShapeDtypeStruct((n_blocks * rows, d), F32),
        grid_spec=grid_spec,
        compiler_params=_params(("arbitrary",)),
        name="experts",
    )(blk_expert, slot_tok, n_valid, next_expert, parity, h, g2.reshape(1, d), w_g, w_u, w_d)


def _combine_kernel(slot1_ref, slot2_ref, h_ref, w_ref, gf_ref, y_hbm, o_ref, ybuf, sems):
    i = pl.program_id(0)
    n_tiles = pl.num_programs(0)
    tm = h_ref.shape[0]

    def gather(tile, buf):
        cps = []
        for r in range(tm):
            cps.append(_row_copy(y_hbm, slot1_ref[tile * tm + r], ybuf.at[buf, 0], r, sems.at[buf]))
            cps.append(_row_copy(y_hbm, slot2_ref[tile * tm + r], ybuf.at[buf, 1], r, sems.at[buf]))
        return cps

    @pl.when(i == 0)
    def _():
        for cp in gather(0, 0):
            cp.start()

    @pl.when(i + 1 < n_tiles)
    def _():
        for cp in gather(i + 1, (i + 1) % 2):
            cp.start()

    buf = i % 2
    for cp in gather(i, buf):
        cp.wait()
    w = w_ref[...]
    h = h_ref[...] + w[:, 0:1] * ybuf[buf, 0] + w[:, 1:2] * ybuf[buf, 1]
    o_ref[...] = h * lax.rsqrt(jnp.mean(h * h, axis=-1, keepdims=True) + NORM_EPS) * gf_ref[...]


def _combine(h, y, slot1, slot2, w, gf):
    t, d = h.shape
    tm = _tile(t, 128)
    grid_spec = pltpu.PrefetchScalarGridSpec(
        num_scalar_prefetch=2,
        grid=(t // tm,),
        in_specs=[
            pl.BlockSpec((tm, d), lambda i, s1, s2: (i, 0)),
            pl.BlockSpec((tm, 2), lambda i, s1, s2: (i, 0)),
            pl.BlockSpec((1, d), lambda i, s1, s2: (0, 0)),
            pl.BlockSpec(memory_space=pl.ANY),
        ],
        out_specs=pl.BlockSpec((tm, d), lambda i, s1, s2: (i, 0)),
        scratch_shapes=[pltpu.VMEM((2, 2, tm, d), F32), pltpu.SemaphoreType.DMA((2,))],
    )
    return pl.pallas_call(
        _combine_kernel,
        out_shape=jax.ShapeDtypeStruct((t, d), F32),
        grid_spec=grid_spec,
        compiler_params=_params(("arbitrary",)),
        name="combine",
    )(slot1, slot2, h, w, gf.reshape(1, d), y)


def _layer(h, batch, seq, norm1_g, w_in, conv_dw_w, conv_dw_b, conv_ln_g, conv_ln_b, w_conv_out, w_attn_out,
           gate_b, w_out, norm2_g, w_rg, b_rg, w_re, b_re, w_exp_gate, w_exp_up, w_exp_down):
    t, d = h.shape
    c = w_conv_out.shape[0]
    aw = w_attn_out.shape[0]
    n_heads = aw // HEAD_DIM
    c1 = 2 * c
    c2 = c1 + 3 * aw

    u = _rmsnorm(h, norm1_g, BF16)
    w_in_b = w_in.astype(BF16)
    z = _proj_call(_glu_proj_kernel, u, [w_in_b[:, :c], w_in_b[:, c:c1]], [], c, F32, "glu_proj")
    qkv = _proj_call(_proj_kernel, u, [w_in_b[:, c1:c2]], [], 3 * aw, BF16, "qkv_proj")
    gates = _proj_call(_gate_proj_kernel, u, [w_in_b[:, c2:]], [gate_b.reshape(1, 2 * d)], 2 * d, BF16, "gate_proj")

    conv = _conv_branch(z, seq, conv_dw_w, conv_dw_b, conv_ln_g, conv_ln_b)
    attn = _moba(qkv, batch, seq, n_heads)

    wr_t = jnp.zeros((ROUTER_ROWS, d), F32).at[:N_GROUPS].set(w_rg.T).at[N_GROUPS:N_GROUPS + N_EXPERTS].set(w_re.T)
    b_r = jnp.zeros((ROUTER_ROWS,), F32).at[:N_GROUPS].set(b_rg).at[N_GROUPS:N_GROUPS + N_EXPERTS].set(b_re)
    h_mid, route, counts = _mix(conv, attn, gates, h, w_conv_out.astype(BF16), w_attn_out.astype(BF16),
                                w_out.astype(BF16), norm2_g, wr_t.astype(BF16), b_r.reshape(ROUTER_ROWS, 1))

    n_assign = 2 * t
    n_blocks = -(-n_assign // MOE_BLOCK) + N_EXPERTS
    counts = counts[:, 0].astype(jnp.int32)
    padded = (counts + MOE_BLOCK - 1) // MOE_BLOCK * MOE_BLOCK
    pad_end = jnp.cumsum(padded)
    pad_start = pad_end - padded
    e1 = route[0].astype(jnp.int32)
    e2 = route[1].astype(jnp.int32)
    slot1 = pad_start[e1] + route[4].astype(jnp.int32)
    slot2 = pad_start[e2] + route[5].astype(jnp.int32)
    tok = jnp.arange(t, dtype=jnp.int32)
    slot_tok = jnp.zeros((n_blocks * MOE_BLOCK,), jnp.int32).at[jnp.concatenate([slot1, slot2])].set(
        jnp.concatenate([tok, tok]))
    blk_start = jnp.arange(n_blocks, dtype=jnp.int32) * MOE_BLOCK
    blk_expert = jnp.minimum(jnp.sum(pad_end[None, :] <= blk_start[:, None], axis=1), N_EXPERTS - 1).astype(jnp.int32)
    n_valid = (pad_end[-1:] // MOE_BLOCK).astype(jnp.int32)
    ids = jnp.arange(N_EXPERTS, dtype=jnp.int32)
    nonempty = counts > 0
    later = jnp.logical_and(ids[None, :] > ids[:, None], nonempty[None, :])
    next_expert = jnp.min(jnp.where(later, ids[None, :], N_EXPERTS), axis=1).astype(jnp.int32)
    parity = (jnp.sum(jnp.logical_and(ids[None, :] < ids[:, None], nonempty[None, :]), axis=1) % 2).astype(jnp.int32)

    y = _experts(h_mid, norm2_g, w_exp_gate, w_exp_up, w_exp_down, blk_expert, slot_tok, n_valid,
                 next_expert, parity)
    weights = jnp.stack([route[2], route[3]], axis=1)
    return h_mid, y, slot1, slot2, weights


def kernel(x, norm1_g, w_in, conv_dw_w, conv_dw_b, conv_ln_g, conv_ln_b, w_conv_out, w_attn_out, gate_b, w_out,
           norm2_g, w_router_group, b_router_group, w_router_expert, b_router_expert, w_exp_gate, w_exp_up,
           w_exp_down, norm_f_g):
    b, s, d = x.shape
    assert norm1_g.shape[0] == 1, "single-layer stack: the combine kernel fuses the final RMSNorm"
    h_mid, y, slot1, slot2, weights = _layer(
        x.reshape(b * s, d), b, s, norm1_g[0], w_in[0], conv_dw_w[0], conv_dw_b[0], conv_ln_g[0], conv_ln_b[0],
        w_conv_out[0], w_attn_out[0], gate_b[0], w_out[0], norm2_g[0], w_router_group[0],
        b_router_group[0], w_router_expert[0], b_router_expert[0], w_exp_gate[0], w_exp_up[0], w_exp_down[0])
    return _combine(h_mid, y, slot1, slot2, weights, norm_f_g).reshape(b, s, d)
```

```python
import functools

import jax
import jax.numpy as jnp
from jax import lax
from jax.experimental import pallas as pl
from jax.experimental.pallas import tpu as pltpu

F32 = jnp.float32
BF16 = jnp.bfloat16

NORM_EPS = 1e-6
NEG_BIG = -1e30
MASKED = -2e30
LOG2_E = 1.4426950408889634
MOBA_GROUP = 4
MOBA_HEADS_PER_STEP = 4
MOBA_BIAS_PIECES = 3
MOBA_SUM_ROWS = 8
CONV_TAPS = 31
HEAD_DIM = 128
MOBA_BLOCK = 256
MOBA_TOPK = 3
N_GROUPS = 8
EXPERTS_PER_GROUP = 8
N_EXPERTS = N_GROUPS * EXPERTS_PER_GROUP
MOE_BLOCK = 256
ROW_RING = 3
ROUTER_ROWS = 128
CONV_HALO = 32
SUBLANES = 8
V7X_VMEM_LIMIT_BYTES = 56 * 1024 * 1024

NT_DIMS = (((1,), (1,)), ((), ()))


def _params(semantics, flags=None):
    return pltpu.CompilerParams(dimension_semantics=semantics, vmem_limit_bytes=V7X_VMEM_LIMIT_BYTES, flags=flags)


def _tile(n, want):
    t = min(n, want)
    while n % t:
        t //= 2
    return t


def _norm_glu_proj_kernel(x_ref, g_ref, wv_ref, wg_ref, z_ref, u_ref):
    x = x_ref[...]
    u = (x * lax.rsqrt(jnp.mean(x * x, axis=-1, keepdims=True) + NORM_EPS) * g_ref[...]).astype(u_ref.dtype)
    u_ref[...] = u
    val = jnp.dot(u, wv_ref[...], preferred_element_type=F32)
    gate = jnp.dot(u, wg_ref[...], preferred_element_type=F32)
    z_ref[...] = val * jax.nn.sigmoid(gate)


def _norm_glu_proj(x, g, w_val, w_gate):
    t, d = x.shape
    c = w_val.shape[1]
    tm = _tile(t, 512)
    const = lambda shape: pl.BlockSpec(shape, lambda i: (0, 0), pipeline_mode=pl.Buffered(1))
    return pl.pallas_call(
        _norm_glu_proj_kernel,
        out_shape=(jax.ShapeDtypeStruct((t, c), F32), jax.ShapeDtypeStruct((t, d), BF16)),
        grid=(t // tm,),
        in_specs=[pl.BlockSpec((tm, d), lambda i: (i, 0)), const((1, d)), const((d, c)), const((d, c))],
        out_specs=(pl.BlockSpec((tm, c), lambda i: (i, 0)), pl.BlockSpec((tm, d), lambda i: (i, 0))),
        compiler_params=_params(("parallel",)),
        name="norm_glu_proj",
    )(x, g.reshape(1, d), w_val, w_gate)


def _proj_kernel(u_ref, w_ref, o_ref):
    o_ref[...] = jnp.dot(u_ref[...], w_ref[...], preferred_element_type=F32).astype(o_ref.dtype)


def _gate_proj_kernel(u_ref, w_ref, b_ref, o_ref):
    y = jnp.dot(u_ref[...], w_ref[...], preferred_element_type=F32) + b_ref[...]
    o_ref[...] = jax.nn.sigmoid(y).astype(o_ref.dtype)


def _proj_call(kernel, u, weights, extra, n_out, out_dtype, name):
    t, k = u.shape
    tm = _tile(t, 1024)
    tn = _tile(n_out, 1024)
    in_specs = [pl.BlockSpec((tm, k), lambda i, j: (i, 0))]
    in_specs += [pl.BlockSpec((k, tn), lambda i, j: (0, j)) for _ in weights]
    in_specs += [pl.BlockSpec((1, tn), lambda i, j: (0, j)) for _ in extra]
    return pl.pallas_call(
        kernel,
        out_shape=jax.ShapeDtypeStruct((t, n_out), out_dtype),
        grid=(t // tm, n_out // tn),
        in_specs=in_specs,
        out_specs=pl.BlockSpec((tm, tn), lambda i, j: (i, j)),
        compiler_params=_params(("parallel", "parallel")),
        name=name,
    )(u, *weights, *extra)


def _conv_kernel(z_ref, halo_ref, w_ref, b_ref, g_ref, beta_ref, o_ref, ext_ref, acc_ref, *, tiles_per_seq):
    i = pl.program_id(0)
    ts, c = z_ref.shape
    seq_start = (i % tiles_per_seq) == 0
    n_lt = c // 128
    span = ts + CONV_HALO - SUBLANES
    for lt in range(n_lt):
        ls = slice(lt * 128, (lt + 1) * 128)
        ext_ref[0, lt, 0:CONV_HALO, :] = jnp.where(seq_start, 0.0, halo_ref[:, ls])
        ext_ref[0, lt, CONV_HALO:, :] = z_ref[:, ls]
        for s in range(1, SUBLANES):
            ext_ref[s, lt, 0:span, :] = ext_ref[0, lt, s:s + span, :]
    rows = 64
    first = CONV_HALO - (CONV_TAPS - 1)
    n_rc = ts // rows

    def chunk(ci, carry):
        lt = ci // n_rc
        r0 = pl.multiple_of((ci % n_rc) * rows, rows)
        accs = [jnp.broadcast_to(b_ref[lt], (rows, 128)), jnp.zeros((rows, 128), F32)]
        for tap in range(CONV_TAPS):
            s = (first + tap) % SUBLANES
            base = pl.multiple_of(r0 + (first + tap - s), SUBLANES)
            accs[tap % 2] = accs[tap % 2] + ext_ref[s, lt, pl.ds(base, rows), :] * w_ref[lt, tap:tap + 1, :]
        acc_ref[lt, pl.ds(r0, rows), :] = accs[0] + accs[1]
        return carry

    lax.fori_loop(0, n_lt * n_rc, chunk, 0)
    y = jnp.concatenate([acc_ref[lt] for lt in range(n_lt)], axis=1)
    mu = jnp.mean(y, axis=-1, keepdims=True)
    yc = y - mu
    var = jnp.mean(yc * yc, axis=-1, keepdims=True)
    yn = yc * lax.rsqrt(var + NORM_EPS) * g_ref[...] + beta_ref[...]
    o_ref[...] = (yn * jax.nn.sigmoid(yn)).astype(o_ref.dtype)


def _conv_branch(z, seq, w_dw, b_dw, ln_g, ln_b):
    t, c = z.shape
    ts = _tile(seq, 256)
    hb = ts // CONV_HALO
    kernel = functools.partial(_conv_kernel, tiles_per_seq=seq // ts)
    row = lambda v: v.reshape(1, c)
    n_lt = c // 128
    w_lt = w_dw.reshape(CONV_TAPS, n_lt, 128).transpose(1, 0, 2)
    b_lt = b_dw.reshape(n_lt, 1, 128)
    return pl.pallas_call(
        kernel,
        out_shape=jax.ShapeDtypeStruct((t, c), BF16),
        grid=(t // ts,),
        in_specs=[
            pl.BlockSpec((ts, c), lambda i: (i, 0)),
            pl.BlockSpec((CONV_HALO, c), lambda i: (jnp.maximum(i * hb - 1, 0), 0)),
            pl.BlockSpec((n_lt, CONV_TAPS, 128), lambda i: (0, 0, 0)),
            pl.BlockSpec((n_lt, 1, 128), lambda i: (0, 0, 0)),
            pl.BlockSpec((1, c), lambda i: (0, 0)),
            pl.BlockSpec((1, c), lambda i: (0, 0)),
        ],
        out_specs=pl.BlockSpec((ts, c), lambda i: (i, 0)),
        scratch_shapes=[pltpu.VMEM((SUBLANES, n_lt, ts + CONV_HALO, 128), F32), pltpu.VMEM((n_lt, ts, 128), F32)],
        compiler_params=_params(("parallel",)),
        name="conv",
    )(z, z, w_lt, b_lt, row(ln_g), row(ln_b))


def _moba_kernel(q_ref, k_ref, v_ref, o_ref, kmean_ref, kaug_ref, vt_ref, sel_ref, mask_ref,
                 *, n_blk, n_heads, hp, scale):
    head0 = pl.program_id(1) * hp
    qb = pl.program_id(2)
    blk = MOBA_BLOCK
    dh = HEAD_DIM
    lanes = [slice(h * dh, (h + 1) * dh) for h in range(hp)]
    slope2 = [jnp.exp2(-8.0 * (head0 + h + 1).astype(F32) / n_heads) * LOG2_E for h in range(hp)]

    @pl.when(qb == 0)
    def _():
        k_local = lax.broadcasted_iota(jnp.int32, (blk, dh), 0).astype(F32)
        col = lax.broadcasted_iota(jnp.int32, (blk, dh), 1)
        extras = []
        for h in range(hp):
            rest = slope2[h] * k_local
            extra = jnp.zeros((blk, dh), F32)
            for piece in range(MOBA_BIAS_PIECES):
                part = rest.astype(BF16).astype(F32)
                extra = jnp.where(col == piece, part, extra)
                rest = rest - part
            extras.append(extra.astype(BF16))

        def per_block(kb, carry):
            st = pl.multiple_of(kb * blk, blk)
            for h in range(hp):
                k = k_ref[pl.ds(st, blk), lanes[h]]
                kmean_ref[h, pl.ds(kb, 1), :] = jnp.mean(k.astype(F32), axis=0, keepdims=True)
                kaug_ref[h, pl.ds(st, blk), 0:dh] = k
                kaug_ref[h, pl.ds(st, blk), dh:2 * dh] = extras[h]
                vt_ref[h, 0:dh, pl.ds(st, blk)] = v_ref[pl.ds(st, blk), lanes[h]].astype(F32).T.astype(BF16)
            return carry
        lax.fori_loop(0, n_blk, per_block, 0)
        ones_row = lax.broadcasted_iota(jnp.int32, (MOBA_SUM_ROWS, vt_ref.shape[2]), 0) == 0
        for h in range(hp):
            vt_ref[h, dh:dh + MOBA_SUM_ROWS, :] = jnp.where(ones_row, 1.0, 0.0).astype(BF16)
        key_i = lax.broadcasted_iota(jnp.int32, (blk, blk), 0)
        query_i = lax.broadcasted_iota(jnp.int32, (blk, blk), 1)
        mask_ref[...] = jnp.where(query_i >= key_i, 0.0, MASKED)

    rows = lax.broadcasted_iota(jnp.int32, (n_blk, blk), 0)
    ones_cols = jnp.where(lax.broadcasted_iota(jnp.int32, (blk, dh), 1) < MOBA_BIAS_PIECES, 1.0, 0.0).astype(BF16)
    q_aug = []
    for h in range(hp):
        q = q_ref[:, lanes[h]]
        gate = lax.dot_general(kmean_ref[h].astype(BF16), q, NT_DIMS, preferred_element_type=F32)
        gate = jnp.where(rows < qb, gate, -jnp.inf)
        sel = jnp.zeros((n_blk, blk), F32)
        for _ in range(MOBA_TOPK):
            top = jnp.max(gate, axis=0, keepdims=True)
            idx = jnp.min(jnp.where(gate == top, rows, n_blk), axis=0, keepdims=True)
            pick = rows == idx
            sel = jnp.where(pick, jnp.where(top > -jnp.inf, 1.0, sel), sel)
            gate = jnp.where(pick, -jnp.inf, gate)
        sel_ref[h] = sel
        qs = (q.astype(F32) * (scale * LOG2_E)).astype(BF16)
        q_aug.append(jnp.concatenate([qs, ones_cols], axis=1))

    def block_scores(h, st):
        return lax.dot_general(kaug_ref[h, pl.ds(st, blk), :], q_aug[h], NT_DIMS, preferred_element_type=F32)

    own_st = pl.multiple_of(qb * blk, blk)
    own = [block_scores(h, own_st) + mask_ref[...] for h in range(hp)]
    carry = []
    for h in range(hp):
        m = jnp.max(own[h], axis=0, keepdims=True)
        p = jnp.exp2(own[h] - m).astype(BF16)
        carry.append((m, jnp.dot(vt_ref[h, :, pl.ds(own_st, blk)], p, preferred_element_type=F32)))

    def per_group(j, carry):
        starts = [pl.multiple_of((j * MOBA_GROUP + g) * blk, blk) for g in range(MOBA_GROUP)]
        chains = [(h, j * MOBA_GROUP + g, starts[g]) for h in range(hp) for g in range(MOBA_GROUP)]
        scores = [block_scores(h, st) for h, _, st in chains]
        stats = []
        for (h, kb, _), s in zip(chains, scores):
            mb = jnp.max(s, axis=0, keepdims=True)
            p = jnp.exp2(s - mb).astype(BF16)
            block_term = slope2[h] * ((kb - qb) * blk).astype(F32)
            mb = jnp.where(sel_ref[h, pl.ds(kb, 1), :] > 0.5, mb + block_term, MASKED)
            stats.append((mb, p))
        pvs = [jnp.dot(vt_ref[h, :, pl.ds(st, blk)], p, preferred_element_type=F32)
               for (h, _, st), (_, p) in zip(chains, stats)]
        out = []
        for h in range(hp):
            m, acc = carry[h]
            mine = [(mb, pv) for (hh, _, _), (mb, _), pv in zip(chains, stats, pvs) if hh == h]
            m_new = m
            for mb, _ in mine:
                m_new = jnp.maximum(m_new, mb)
            acc = jnp.exp2(m - m_new) * acc
            for mb, pv in mine:
                acc = acc + jnp.exp2(mb - m_new) * pv
            out.append((m_new, acc))
        return tuple(out)

    n_groups = (qb + MOBA_GROUP - 1) // MOBA_GROUP
    final = lax.fori_loop(0, n_groups, per_group, tuple(carry))
    for h in range(hp):
        _, acc = final[h]
        o_ref[:, lanes[h]] = (acc[0:dh] / acc[dh:dh + 1]).T.astype(o_ref.dtype)


def _moba(qkv, batch, seq, n_heads):
    t = qkv.shape[0]
    blk = MOBA_BLOCK
    n_blk = seq // blk
    hp = _tile(n_heads, MOBA_HEADS_PER_STEP)
    assert n_blk % MOBA_GROUP == 0
    kernel = functools.partial(_moba_kernel, n_blk=n_blk, n_heads=n_heads, hp=hp, scale=HEAD_DIM ** -0.5)
    wide = hp * HEAD_DIM
    hsteps = n_heads // hp
    return pl.pallas_call(
        kernel,
        out_shape=jax.ShapeDtypeStruct((t, n_heads * HEAD_DIM), BF16),
        grid=(batch, hsteps, n_blk),
        in_specs=[
            pl.BlockSpec((blk, wide), lambda b, h, i: (b * n_blk + i, h)),
            pl.BlockSpec((seq, wide), lambda b, h, i: (b, hsteps + h), pipeline_mode=pl.Buffered(1)),
            pl.BlockSpec((seq, wide), lambda b, h, i: (b, 2 * hsteps + h), pipeline_mode=pl.Buffered(1)),
        ],
        out_specs=pl.BlockSpec((blk, wide), lambda b, h, i: (b * n_blk + i, h)),
        scratch_shapes=[
            pltpu.VMEM((hp, n_blk, HEAD_DIM), F32),
            pltpu.VMEM((hp, seq, 2 * HEAD_DIM), BF16),
            pltpu.VMEM((hp, HEAD_DIM + MOBA_SUM_ROWS, seq), BF16),
            pltpu.VMEM((hp, n_blk, blk), F32),
            pltpu.VMEM((blk, blk), F32),
        ],
        compiler_params=_params(("parallel", "parallel", "arbitrary")),
        name="moba",
    )(qkv, qkv, qkv)


def _mix_kernel(c_ref, a_ref, gates_ref, x_ref, wpw_ref, wao_ref, wout_ref, g2_ref, wrt_ref, br_ref,
                h_ref, route_ref, cnt_ref, carry_ref):
    i = pl.program_id(0)
    tm, d = x_ref.shape

    @pl.when(i == 0)
    def _():
        carry_ref[...] = jnp.zeros_like(carry_ref)

    yc = jnp.dot(c_ref[...], wpw_ref[...], preferred_element_type=F32)
    ya = jnp.dot(a_ref[...], wao_ref[...], preferred_element_type=F32)
    merged = gates_ref[:, :d].astype(F32) * yc + gates_ref[:, d:].astype(F32) * ya
    h = x_ref[...] + jnp.dot(merged.astype(BF16), wout_ref[...], preferred_element_type=F32)
    h_ref[...] = h
    hn = h * lax.rsqrt(jnp.mean(h * h, axis=-1, keepdims=True) + NORM_EPS) * g2_ref[...]
    logits = lax.dot_general(wrt_ref[...], hn.astype(BF16), NT_DIMS, preferred_element_type=F32) + br_ref[...]

    iota8 = lax.broadcasted_iota(jnp.int32, (N_GROUPS, tm), 0)
    gl = logits[0:N_GROUPS]
    gmax = jnp.max(gl, axis=0, keepdims=True)
    g_p = 1.0 / jnp.sum(jnp.exp(gl - gmax), axis=0, keepdims=True)
    g_idx = jnp.min(jnp.where(gl == gmax, iota8, N_GROUPS), axis=0, keepdims=True)
    esel = jnp.zeros((EXPERTS_PER_GROUP, tm), F32)
    for g in range(N_GROUPS):
        lo = N_GROUPS + g * EXPERTS_PER_GROUP
        esel = jnp.where(g_idx == g, logits[lo:lo + EXPERTS_PER_GROUP], esel)
    v1 = jnp.max(esel, axis=0, keepdims=True)
    i1 = jnp.min(jnp.where(esel == v1, iota8, EXPERTS_PER_GROUP), axis=0, keepdims=True)
    rest = jnp.where(iota8 == i1, -jnp.inf, esel)
    v2 = jnp.max(rest, axis=0, keepdims=True)
    i2 = jnp.min(jnp.where(rest == v2, iota8, EXPERTS_PER_GROUP), axis=0, keepdims=True)
    r = jnp.exp(v2 - v1)
    w1 = g_p / (1.0 + r)
    w2 = g_p * r / (1.0 + r)
    e1 = g_idx * EXPERTS_PER_GROUP + i1
    e2 = g_idx * EXPERTS_PER_GROUP + i2

    iota_e = lax.broadcasted_iota(jnp.int32, (N_EXPERTS, tm), 0)
    oh1 = iota_e == e1
    oh2 = iota_e == e2
    onehot = jnp.concatenate([jnp.where(oh1, 1.0, 0.0), jnp.where(oh2, 1.0, 0.0)], axis=1)
    before = (lax.broadcasted_iota(jnp.int32, (2 * tm, 2 * tm), 0)
              < lax.broadcasted_iota(jnp.int32, (2 * tm, 2 * tm), 1))
    prefix = jnp.dot(onehot.astype(BF16), jnp.where(before, 1.0, 0.0).astype(BF16), preferred_element_type=F32)
    carry = carry_ref[...]
    base = prefix + carry[:, 0:1]
    rank1 = jnp.sum(jnp.where(oh1, base[:, :tm], 0.0), axis=0, keepdims=True)
    rank2 = jnp.sum(jnp.where(oh2, base[:, tm:], 0.0), axis=0, keepdims=True)
    total = carry + jnp.sum(onehot, axis=1, keepdims=True)
    carry_ref[...] = total
    cnt_ref[...] = total

    route_ref[0:1, :] = e1.astype(F32)
    route_ref[1:2, :] = e2.astype(F32)
    route_ref[2:3, :] = w1
    route_ref[3:4, :] = w2
    route_ref[4:5, :] = rank1
    route_ref[5:6, :] = rank2
    route_ref[6:8, :] = jnp.zeros((2, tm), F32)


def _mix(c, a, gates, x, w_pw, w_ao, w_out, g2, wr_t, b_r):
    t, d = x.shape
    cw = c.shape[1]
    aw = a.shape[1]
    tm = _tile(t, 256)
    const = lambda shape: pl.BlockSpec(shape, lambda i: (0, 0), pipeline_mode=pl.Buffered(1))
    return pl.pallas_call(
        _mix_kernel,
        out_shape=(
            jax.ShapeDtypeStruct((t, d), F32),
            jax.ShapeDtypeStruct((8, t), F32),
            jax.ShapeDtypeStruct((N_EXPERTS, 128), F32),
        ),
        grid=(t // tm,),
        in_specs=[
            pl.BlockSpec((tm, cw), lambda i: (i, 0)),
            pl.BlockSpec((tm, aw), lambda i: (i, 0)),
            pl.BlockSpec((tm, 2 * d), lambda i: (i, 0)),
            pl.BlockSpec((tm, d), lambda i: (i, 0)),
            const((cw, d)),
            const((aw, d)),
            const((d, d)),
            const((1, d)),
            const((ROUTER_ROWS, d)),
            const((ROUTER_ROWS, 1)),
        ],
        out_specs=(
            pl.BlockSpec((tm, d), lambda i: (i, 0)),
            pl.BlockSpec((8, tm), lambda i: (0, i)),
            pl.BlockSpec((N_EXPERTS, 128), lambda i: (0, 0)),
        ),
        scratch_shapes=[pltpu.VMEM((N_EXPERTS, 128), F32)],
        compiler_params=_params(("arbitrary",)),
        name="mix",
    )(c, a, gates, x, w_pw, w_ao, w_out, g2.reshape(1, d), wr_t, b_r)


def _row_copy(src_hbm, row, dst_vmem, dst_row, sem):
    return pltpu.make_async_copy(src_hbm.at[pl.ds(row, 1), :], dst_vmem.at[pl.ds(dst_row, 1), :], sem)


def _experts_kernel(blk_expert_ref, slot_tok_ref, n_valid_ref, next_expert_ref, parity_ref,
                    h_hbm, g2_ref, wg_hbm, wu_hbm, wd_hbm, y_ref,
                    xbuf, xsem, wg_f, wu_f, wd_f, wsem, wg_b, wu_b, wd_b):
    n = pl.program_id(0)
    n_valid = n_valid_ref[0]
    rows = MOE_BLOCK
    de = wg_b.shape[1]

    def weight_copies(e, slot):
        return [pltpu.make_async_copy(wg_hbm.at[e], wg_f.at[slot], wsem.at[slot]),
                pltpu.make_async_copy(wu_hbm.at[e], wu_f.at[slot], wsem.at[slot]),
                pltpu.make_async_copy(wd_hbm.at[e], wd_f.at[slot], wsem.at[slot])]

    def row_copy(block, r, slot):
        return _row_copy(h_hbm, slot_tok_ref[block * rows + r], xbuf.at[slot], r, xsem.at[slot])

    def start_weights(e, slot):
        for cp in weight_copies(e, slot):
            cp.start(priority=1)

    @pl.when(n == 0)
    def _():
        start_weights(blk_expert_ref[0], 0)
        for r in range(rows):
            row_copy(0, r, 0).start()
        for r in range(rows):
            row_copy(jnp.minimum(1, n_valid - 1), r, 1).start()

    @pl.when(n >= n_valid)
    def _():
        y_ref[...] = jnp.zeros_like(y_ref)

    @pl.when(n < n_valid)
    def _():
        e = blk_expert_ref[n]
        wslot = parity_ref[e]

        @pl.when(jnp.logical_or(n == 0, blk_expert_ref[jnp.maximum(n - 1, 0)] != e))
        def _():
            nxt = next_expert_ref[e]

            @pl.when(nxt < N_EXPERTS)
            def _():
                start_weights(nxt, 1 - wslot)

            for cp in weight_copies(e, wslot):
                cp.wait()
            wg_b[...] = wg_f[wslot].astype(BF16)
            wu_b[...] = wu_f[wslot].astype(BF16)
            wd_b[...] = wd_f[wslot].astype(BF16)

        slot = n % ROW_RING
        for r in range(rows):
            row_copy(n, r, slot).wait()
        nb = jnp.minimum(n + 2, n_valid - 1)
        nslot = (n + 2) % ROW_RING
        col = min(256, de)
        pieces = 2 * (de // col) + y_ref.shape[1] // col
        per_piece = -(-rows // pieces)
        issued = [0]

        def issue_some():
            lo = issued[0]
            hi = min(rows, lo + per_piece)
            for r in range(lo, hi):
                row_copy(nb, r, nslot).start()
            issued[0] = hi

        x = xbuf[slot]
        xn = x * lax.rsqrt(jnp.mean(x * x, axis=-1, keepdims=True) + NORM_EPS) * g2_ref[...]
        xb = xn.astype(BF16)
        hidden = []
        for c in range(de // col):
            cs = slice(c * col, (c + 1) * col)
            gate = jnp.dot(xb, wg_b[:, cs], preferred_element_type=F32)
            issue_some()
            up = jnp.dot(xb, wu_b[:, cs], preferred_element_type=F32)
            issue_some()
            hidden.append((gate * jax.nn.sigmoid(gate) * up).astype(BF16))
        hidden = jnp.concatenate(hidden, axis=1)
        for c in range(y_ref.shape[1] // col):
            cs = slice(c * col, (c + 1) * col)
            y_ref[:, cs] = jnp.dot(hidden, wd_b[:, cs], preferred_element_type=F32)
            issue_some()
        assert issued[0] == rows

        @pl.when(n == n_valid - 1)
        def _():
            for ahead in (1, 2):
                for r in range(rows):
                    row_copy(n, r, (n + ahead) % ROW_RING).wait()


def _experts(h, g2, w_g, w_u, w_d, blk_expert, slot_tok, n_valid, next_expert, parity):
    t, d = h.shape
    de = w_g.shape[2]
    n_blocks = blk_expert.shape[0]
    rows = MOE_BLOCK
    grid_spec = pltpu.PrefetchScalarGridSpec(
        num_scalar_prefetch=5,
        grid=(n_blocks,),
        in_specs=[
            pl.BlockSpec(memory_space=pl.ANY),
            pl.BlockSpec((1, d), lambda n, *_: (0, 0)),
            pl.BlockSpec(memory_space=pl.ANY),
            pl.BlockSpec(memory_space=pl.ANY),
            pl.BlockSpec(memory_space=pl.ANY),
        ],
        out_specs=pl.BlockSpec((rows, d), lambda n, *_: (n, 0)),
        scratch_shapes=[
            pltpu.VMEM((ROW_RING, rows, d), F32),
            pltpu.SemaphoreType.DMA((ROW_RING,)),
            pltpu.VMEM((2, d, de), F32),
            pltpu.VMEM((2, d, de), F32),
            pltpu.VMEM((2, de, d), F32),
            pltpu.SemaphoreType.DMA((2,)),
            pltpu.VMEM((d, de), BF16),
            pltpu.VMEM((d, de), BF16),
            pltpu.VMEM((de, d), BF16),
        ],
    )
    return pl.pallas_call(
        _experts_kernel,
        out_shape=jax.ShapeDtypeStruct((n_blocks * rows, d), F32),
        grid_spec=grid_spec,
        compiler_params=_params(("arbitrary",)),
        name="experts",
    )(blk_expert, slot_tok, n_valid, next_expert, parity, h, g2.reshape(1, d), w_g, w_u, w_d)


def _combine_kernel(slot1_ref, slot2_ref, h_ref, w_ref, gf_ref, y_hbm, o_ref, ybuf, sems):
    i = pl.program_id(0)
    n_tiles = pl.num_programs(0)
    tm = h_ref.shape[0]

    def gather(tile, buf):
        cps = []
        for r in range(tm):
            cps.append(_row_copy(y_hbm, slot1_ref[tile * tm + r], ybuf.at[buf, 0], r, sems.at[buf]))
            cps.append(_row_copy(y_hbm, slot2_ref[tile * tm + r], ybuf.at[buf, 1], r, sems.at[buf]))
        return cps

    def start_all(copies):
        for k, cp in enumerate(copies):
            cp.start(priority=k % 2)

    @pl.when(i == 0)
    def _():
        start_all(gather(0, 0))

    @pl.when(i + 1 < n_tiles)
    def _():
        start_all(gather(i + 1, (i + 1) % 2))

    buf = i % 2
    for cp in gather(i, buf):
        cp.wait()
    w = w_ref[...]
    h = h_ref[...] + w[:, 0:1] * ybuf[buf, 0] + w[:, 1:2] * ybuf[buf, 1]
    o_ref[...] = h * lax.rsqrt(jnp.mean(h * h, axis=-1, keepdims=True) + NORM_EPS) * gf_ref[...]


def _combine(h, y, slot1, slot2, w, gf):
    t, d = h.shape
    tm = _tile(t, 128)
    grid_spec = pltpu.PrefetchScalarGridSpec(
        num_scalar_prefetch=2,
        grid=(t // tm,),
        in_specs=[
            pl.BlockSpec((tm, d), lambda i, s1, s2: (i, 0)),
            pl.BlockSpec((tm, 2), lambda i, s1, s2: (i, 0)),
            pl.BlockSpec((1, d), lambda i, s1, s2: (0, 0)),
            pl.BlockSpec(memory_space=pl.ANY),
        ],
        out_specs=pl.BlockSpec((tm, d), lambda i, s1, s2: (i, 0)),
        scratch_shapes=[pltpu.VMEM((2, 2, tm, d), F32), pltpu.SemaphoreType.DMA((2,))],
    )
    return pl.pallas_call(
        _combine_kernel,
        out_shape=jax.ShapeDtypeStruct((t, d), F32),
        grid_spec=grid_spec,
        compiler_params=_params(("arbitrary",)),
        name="combine",
    )(slot1, slot2, h, w, gf.reshape(1, d), y)


def _layer(h, batch, seq, norm1_g, w_in, conv_dw_w, conv_dw_b, conv_ln_g, conv_ln_b, w_conv_out, w_attn_out,
           gate_b, w_out, norm2_g, w_rg, b_rg, w_re, b_re, w_exp_gate, w_exp_up, w_exp_down):
    t, d = h.shape
    c = w_conv_out.shape[0]
    aw = w_attn_out.shape[0]
    n_heads = aw // HEAD_DIM
    c1 = 2 * c
    c2 = c1 + 3 * aw

    w_in_b = w_in.astype(BF16)
    z, u = _norm_glu_proj(h, norm1_g, w_in_b[:, :c], w_in_b[:, c:c1])
    qkv = _proj_call(_proj_kernel, u, [w_in_b[:, c1:c2]], [], 3 * aw, BF16, "qkv_proj")
    gates = _proj_call(_gate_proj_kernel, u, [w_in_b[:, c2:]], [gate_b.reshape(1, 2 * d)], 2 * d, BF16, "gate_proj")

    conv = _conv_branch(z, seq, conv_dw_w, conv_dw_b, conv_ln_g, conv_ln_b)
    attn = _moba(qkv, batch, seq, n_heads)

    wr_t = jnp.zeros((ROUTER_ROWS, d), F32).at[:N_GROUPS].set(w_rg.T).at[N_GROUPS:N_GROUPS + N_EXPERTS].set(w_re.T)
    b_r = jnp.zeros((ROUTER_ROWS,), F32).at[:N_GROUPS].set(b_rg).at[N_GROUPS:N_GROUPS + N_EXPERTS].set(b_re)
    h_mid, route, counts = _mix(conv, attn, gates, h, w_conv_out.astype(BF16), w_attn_out.astype(BF16),
                                w_out.astype(BF16), norm2_g, wr_t.astype(BF16), b_r.reshape(ROUTER_ROWS, 1))

    n_assign = 2 * t
    n_blocks = -(-n_assign // MOE_BLOCK) + N_EXPERTS
    counts = counts[:, 0].astype(jnp.int32)
    padded = (counts + MOE_BLOCK - 1) // MOE_BLOCK * MOE_BLOCK
    pad_end = jnp.cumsum(padded)
    pad_start = pad_end - padded
    e1 = route[0].astype(jnp.int32)
    e2 = route[1].astype(jnp.int32)
    slot1 = pad_start[e1] + route[4].astype(jnp.int32)
    slot2 = pad_start[e2] + route[5].astype(jnp.int32)
    tok = jnp.arange(t, dtype=jnp.int32)
    slot_tok = jnp.zeros((n_blocks * MOE_BLOCK,), jnp.int32).at[jnp.concatenate([slot1, slot2])].set(
        jnp.concatenate([tok, tok]))
    blk_start = jnp.arange(n_blocks, dtype=jnp.int32) * MOE_BLOCK
    blk_expert = jnp.minimum(jnp.sum(pad_end[None, :] <= blk_start[:, None], axis=1), N_EXPERTS - 1).astype(jnp.int32)
    n_valid = (pad_end[-1:] // MOE_BLOCK).astype(jnp.int32)
    ids = jnp.arange(N_EXPERTS, dtype=jnp.int32)
    nonempty = counts > 0
    later = jnp.logical_and(ids[None, :] > ids[:, None], nonempty[None, :])
    next_expert = jnp.min(jnp.where(later, ids[None, :], N_EXPERTS), axis=1).astype(jnp.int32)
    parity = (jnp.sum(jnp.logical_and(ids[None, :] < ids[:, None], nonempty[None, :]), axis=1) % 2).astype(jnp.int32)

    y = _experts(h_mid, norm2_g, w_exp_gate, w_exp_up, w_exp_down, blk_expert, slot_tok, n_valid,
                 next_expert, parity)
    weights = jnp.stack([route[2], route[3]], axis=1)
    return h_mid, y, slot1, slot2, weights


def kernel(x, norm1_g, w_in, conv_dw_w, conv_dw_b, conv_ln_g, conv_ln_b, w_conv_out, w_attn_out, gate_b, w_out,
           norm2_g, w_router_group, b_router_group, w_router_expert, b_router_expert, w_exp_gate, w_exp_up,
           w_exp_down, norm_f_g):
    b, s, d = x.shape
    assert norm1_g.shape[0] == 1, "single-layer stack: the combine kernel fuses the final RMSNorm"
    h_mid, y, slot1, slot2, weights = _layer(
        x.reshape(b * s, d), b, s, norm1_g[0], w_in[0], conv_dw_w[0], conv_dw_b[0], conv_ln_g[0], conv_ln_b[0],
        w_conv_out[0], w_attn_out[0], gate_b[0], w_out[0], norm2_g[0], w_router_group[0],
        b_router_group[0], w_router_expert[0], b_router_expert[0], w_exp_gate[0], w_exp_up[0], w_exp_down[0])
    return _combine(h_mid, y, slot1, slot2, weights, norm_f_g).reshape(b, s, d)
```

```python
import functools

import jax
import jax.numpy as jnp
from jax import lax
from jax.experimental import pallas as pl
from jax.experimental.pallas import tpu as pltpu

F32 = jnp.float32
BF16 = jnp.bfloat16

NORM_EPS = 1e-6
NEG_BIG = -1e30
MASKED = -2e30
LOG2_E = 1.4426950408889634
MOBA_GROUP = 4
MOBA_HEADS_PER_STEP = 4
MOBA_BIAS_PIECES = 3
MOBA_SUM_ROWS = 8
CONV_TAPS = 31
HEAD_DIM = 128
MOBA_BLOCK = 256
MOBA_TOPK = 3
N_GROUPS = 8
EXPERTS_PER_GROUP = 8
N_EXPERTS = N_GROUPS * EXPERTS_PER_GROUP
MOE_BLOCK = 256
ROUTER_ROWS = 128
CONV_HALO = 32
SUBLANES = 8
V7X_VMEM_LIMIT_BYTES = 56 * 1024 * 1024

NT_DIMS = (((1,), (1,)), ((), ()))


def _params(semantics, flags=None):
    return pltpu.CompilerParams(dimension_semantics=semantics, vmem_limit_bytes=V7X_VMEM_LIMIT_BYTES, flags=flags)


def _tile(n, want):
    t = min(n, want)
    while n % t:
        t //= 2
    return t


def _norm_glu_proj_kernel(x_ref, g_ref, wv_ref, wg_ref, z_ref, u_ref):
    x = x_ref[...]
    u = (x * lax.rsqrt(jnp.mean(x * x, axis=-1, keepdims=True) + NORM_EPS) * g_ref[...]).astype(u_ref.dtype)
    u_ref[...] = u
    val = jnp.dot(u, wv_ref[...], preferred_element_type=F32)
    gate = jnp.dot(u, wg_ref[...], preferred_element_type=F32)
    z_ref[...] = val * jax.nn.sigmoid(gate)


def _norm_glu_proj(x, g, w_val, w_gate):
    t, d = x.shape
    c = w_val.shape[1]
    tm = _tile(t, 512)
    const = lambda shape: pl.BlockSpec(shape, lambda i: (0, 0), pipeline_mode=pl.Buffered(1))
    return pl.pallas_call(
        _norm_glu_proj_kernel,
        out_shape=(jax.ShapeDtypeStruct((t, c), F32), jax.ShapeDtypeStruct((t, d), BF16)),
        grid=(t // tm,),
        in_specs=[pl.BlockSpec((tm, d), lambda i: (i, 0)), const((1, d)), const((d, c)), const((d, c))],
        out_specs=(pl.BlockSpec((tm, c), lambda i: (i, 0)), pl.BlockSpec((tm, d), lambda i: (i, 0))),
        compiler_params=_params(("parallel",)),
        name="norm_glu_proj",
    )(x, g.reshape(1, d), w_val, w_gate)


def _proj_kernel(u_ref, w_ref, o_ref):
    o_ref[...] = jnp.dot(u_ref[...], w_ref[...], preferred_element_type=F32).astype(o_ref.dtype)


def _gate_proj_kernel(u_ref, w_ref, b_ref, o_ref):
    y = jnp.dot(u_ref[...], w_ref[...], preferred_element_type=F32) + b_ref[...]
    o_ref[...] = jax.nn.sigmoid(y).astype(o_ref.dtype)


def _proj_call(kernel, u, weights, extra, n_out, out_dtype, name):
    t, k = u.shape
    tm = _tile(t, 1024)
    tn = _tile(n_out, 1024)
    in_specs = [pl.BlockSpec((tm, k), lambda i, j: (i, 0))]
    in_specs += [pl.BlockSpec((k, tn), lambda i, j: (0, j)) for _ in weights]
    in_specs += [pl.BlockSpec((1, tn), lambda i, j: (0, j)) for _ in extra]
    return pl.pallas_call(
        kernel,
        out_shape=jax.ShapeDtypeStruct((t, n_out), out_dtype),
        grid=(t // tm, n_out // tn),
        in_specs=in_specs,
        out_specs=pl.BlockSpec((tm, tn), lambda i, j: (i, j)),
        compiler_params=_params(("parallel", "parallel")),
        name=name,
    )(u, *weights, *extra)


def _conv_kernel(z_ref, halo_ref, w_ref, b_ref, g_ref, beta_ref, o_ref, ext_ref, acc_ref, *, tiles_per_seq):
    i = pl.program_id(0)
    ts, c = z_ref.shape
    seq_start = (i % tiles_per_seq) == 0
    n_lt = c // 128
    span = ts + CONV_HALO - SUBLANES
    for lt in range(n_lt):
        ls = slice(lt * 128, (lt + 1) * 128)
        ext_ref[0, lt, 0:CONV_HALO, :] = jnp.where(seq_start, 0.0, halo_ref[:, ls])
        ext_ref[0, lt, CONV_HALO:, :] = z_ref[:, ls]
        for s in range(1, SUBLANES):
            ext_ref[s, lt, 0:span, :] = ext_ref[0, lt, s:s + span, :]
    rows = 64
    first = CONV_HALO - (CONV_TAPS - 1)
    n_rc = ts // rows

    def chunk(ci, carry):
        lt = ci // n_rc
        r0 = pl.multiple_of((ci % n_rc) * rows, rows)
        accs = [jnp.broadcast_to(b_ref[lt], (rows, 128)), jnp.zeros((rows, 128), F32)]
        for tap in range(CONV_TAPS):
            s = (first + tap) % SUBLANES
            base = pl.multiple_of(r0 + (first + tap - s), SUBLANES)
            accs[tap % 2] = accs[tap % 2] + ext_ref[s, lt, pl.ds(base, rows), :] * w_ref[lt, tap:tap + 1, :]
        acc_ref[lt, pl.ds(r0, rows), :] = accs[0] + accs[1]
        return carry

    lax.fori_loop(0, n_lt * n_rc, chunk, 0)
    y = jnp.concatenate([acc_ref[lt] for lt in range(n_lt)], axis=1)
    mu = jnp.mean(y, axis=-1, keepdims=True)
    yc = y - mu
    var = jnp.mean(yc * yc, axis=-1, keepdims=True)
    yn = yc * lax.rsqrt(var + NORM_EPS) * g_ref[...] + beta_ref[...]
    o_ref[...] = (yn * jax.nn.sigmoid(yn)).astype(o_ref.dtype)


def _conv_branch(z, seq, w_dw, b_dw, ln_g, ln_b):
    t, c = z.shape
    ts = _tile(seq, 256)
    hb = ts // CONV_HALO
    kernel = functools.partial(_conv_kernel, tiles_per_seq=seq // ts)
    row = lambda v: v.reshape(1, c)
    n_lt = c // 128
    w_lt = w_dw.reshape(CONV_TAPS, n_lt, 128).transpose(1, 0, 2)
    b_lt = b_dw.reshape(n_lt, 1, 128)
    return pl.pallas_call(
        kernel,
        out_shape=jax.ShapeDtypeStruct((t, c), BF16),
        grid=(t // ts,),
        in_specs=[
            pl.BlockSpec((ts, c), lambda i: (i, 0)),
            pl.BlockSpec((CONV_HALO, c), lambda i: (jnp.maximum(i * hb - 1, 0), 0)),
            pl.BlockSpec((n_lt, CONV_TAPS, 128), lambda i: (0, 0, 0)),
            pl.BlockSpec((n_lt, 1, 128), lambda i: (0, 0, 0)),
            pl.BlockSpec((1, c), lambda i: (0, 0)),
            pl.BlockSpec((1, c), lambda i: (0, 0)),
        ],
        out_specs=pl.BlockSpec((ts, c), lambda i: (i, 0)),
        scratch_shapes=[pltpu.VMEM((SUBLANES, n_lt, ts + CONV_HALO, 128), F32), pltpu.VMEM((n_lt, ts, 128), F32)],
        compiler_params=_params(("parallel",)),
        name="conv",
    )(z, z, w_lt, b_lt, row(ln_g), row(ln_b))


def _moba_kernel(q_ref, k_ref, v_ref, o_ref, kmean_ref, kaug_ref, vt_ref, sel_ref, mask_ref,
                 *, n_blk, n_heads, hp, scale):
    head0 = pl.program_id(1) * hp
    qb = pl.program_id(2)
    blk = MOBA_BLOCK
    dh = HEAD_DIM
    lanes = [slice(h * dh, (h + 1) * dh) for h in range(hp)]
    slope2 = [jnp.exp2(-8.0 * (head0 + h + 1).astype(F32) / n_heads) * LOG2_E for h in range(hp)]

    @pl.when(qb == 0)
    def _():
        k_local = lax.broadcasted_iota(jnp.int32, (blk, dh), 0).astype(F32)
        col = lax.broadcasted_iota(jnp.int32, (blk, dh), 1)
        extras = []
        for h in range(hp):
            rest = slope2[h] * k_local
            extra = jnp.zeros((blk, dh), F32)
            for piece in range(MOBA_BIAS_PIECES):
                part = rest.astype(BF16).astype(F32)
                extra = jnp.where(col == piece, part, extra)
                rest = rest - part
            extras.append(extra.astype(BF16))

        def per_block(kb, carry):
            st = pl.multiple_of(kb * blk, blk)
            for h in range(hp):
                k = k_ref[pl.ds(st, blk), lanes[h]]
                kmean_ref[h, pl.ds(kb, 1), :] = jnp.mean(k.astype(F32), axis=0, keepdims=True)
                kaug_ref[h, pl.ds(st, blk), 0:dh] = k
                kaug_ref[h, pl.ds(st, blk), dh:2 * dh] = extras[h]
                vt_ref[h, 0:dh, pl.ds(st, blk)] = v_ref[pl.ds(st, blk), lanes[h]].astype(F32).T.astype(BF16)
            return carry
        lax.fori_loop(0, n_blk, per_block, 0)
        ones_row = lax.broadcasted_iota(jnp.int32, (MOBA_SUM_ROWS, vt_ref.shape[2]), 0) == 0
        for h in range(hp):
            vt_ref[h, dh:dh + MOBA_SUM_ROWS, :] = jnp.where(ones_row, 1.0, 0.0).astype(BF16)
        key_i = lax.broadcasted_iota(jnp.int32, (blk, blk), 0)
        query_i = lax.broadcasted_iota(jnp.int32, (blk, blk), 1)
        mask_ref[...] = jnp.where(query_i >= key_i, 0.0, MASKED)

    rows = lax.broadcasted_iota(jnp.int32, (n_blk, blk), 0)
    ones_cols = jnp.where(lax.broadcasted_iota(jnp.int32, (blk, dh), 1) < MOBA_BIAS_PIECES, 1.0, 0.0).astype(BF16)
    q_aug = []
    for h in range(hp):
        q = q_ref[:, lanes[h]]
        gate = lax.dot_general(kmean_ref[h].astype(BF16), q, NT_DIMS, preferred_element_type=F32)
        gate = jnp.where(rows < qb, gate, -jnp.inf)
        sel = jnp.zeros((n_blk, blk), F32)
        for _ in range(MOBA_TOPK):
            top = jnp.max(gate, axis=0, keepdims=True)
            idx = jnp.min(jnp.where(gate == top, rows, n_blk), axis=0, keepdims=True)
            pick = rows == idx
            sel = jnp.where(pick, jnp.where(top > -jnp.inf, 1.0, sel), sel)
            gate = jnp.where(pick, -jnp.inf, gate)
        sel_ref[h] = sel
        qs = (q.astype(F32) * (scale * LOG2_E)).astype(BF16)
        q_aug.append(jnp.concatenate([qs, ones_cols], axis=1))

    def block_scores(h, st):
        return lax.dot_general(kaug_ref[h, pl.ds(st, blk), :], q_aug[h], NT_DIMS, preferred_element_type=F32)

    own_st = pl.multiple_of(qb * blk, blk)
    own = [block_scores(h, own_st) + mask_ref[...] for h in range(hp)]
    carry = []
    for h in range(hp):
        m = jnp.max(own[h], axis=0, keepdims=True)
        p = jnp.exp2(own[h] - m).astype(BF16)
        carry.append((m, jnp.dot(vt_ref[h, :, pl.ds(own_st, blk)], p, preferred_element_type=F32)))

    def per_group(j, carry):
        starts = [pl.multiple_of((j * MOBA_GROUP + g) * blk, blk) for g in range(MOBA_GROUP)]
        chains = [(h, j * MOBA_GROUP + g, starts[g]) for h in range(hp) for g in range(MOBA_GROUP)]
        scores = [block_scores(h, st) for h, _, st in chains]
        stats = []
        for (h, kb, _), s in zip(chains, scores):
            mb = jnp.max(s, axis=0, keepdims=True)
            p = jnp.exp2(s - mb).astype(BF16)
            block_term = slope2[h] * ((kb - qb) * blk).astype(F32)
            mb = jnp.where(sel_ref[h, pl.ds(kb, 1), :] > 0.5, mb + block_term, MASKED)
            stats.append((mb, p))
        pvs = [jnp.dot(vt_ref[h, :, pl.ds(st, blk)], p, preferred_element_type=F32)
               for (h, _, st), (_, p) in zip(chains, stats)]
        out = []
        for h in range(hp):
            m, acc = carry[h]
            mine = [(mb, pv) for (hh, _, _), (mb, _), pv in zip(chains, stats, pvs) if hh == h]
            m_new = m
            for mb, _ in mine:
                m_new = jnp.maximum(m_new, mb)
            acc = jnp.exp2(m - m_new) * acc
            for mb, pv in mine:
                acc = acc + jnp.exp2(mb - m_new) * pv
            out.append((m_new, acc))
        return tuple(out)

    n_groups = (qb + MOBA_GROUP - 1) // MOBA_GROUP
    final = lax.fori_loop(0, n_groups, per_group, tuple(carry))
    for h in range(hp):
        _, acc = final[h]
        o_ref[:, lanes[h]] = (acc[0:dh] / acc[dh:dh + 1]).T.astype(o_ref.dtype)


def _moba(qkv, batch, seq, n_heads):
    t = qkv.shape[0]
    blk = MOBA_BLOCK
    n_blk = seq // blk
    hp = _tile(n_heads, MOBA_HEADS_PER_STEP)
    assert n_blk % MOBA_GROUP == 0
    kernel = functools.partial(_moba_kernel, n_blk=n_blk, n_heads=n_heads, hp=hp, scale=HEAD_DIM ** -0.5)
    wide = hp * HEAD_DIM
    hsteps = n_heads // hp
    return pl.pallas_call(
        kernel,
        out_shape=jax.ShapeDtypeStruct((t, n_heads * HEAD_DIM), BF16),
        grid=(batch, hsteps, n_blk),
        in_specs=[
            pl.BlockSpec((blk, wide), lambda b, h, i: (b * n_blk + i, h)),
            pl.BlockSpec((seq, wide), lambda b, h, i: (b, hsteps + h), pipeline_mode=pl.Buffered(1)),
            pl.BlockSpec((seq, wide), lambda b, h, i: (b, 2 * hsteps + h), pipeline_mode=pl.Buffered(1)),
        ],
        out_specs=pl.BlockSpec((blk, wide), lambda b, h, i: (b * n_blk + i, h)),
        scratch_shapes=[
            pltpu.VMEM((hp, n_blk, HEAD_DIM), F32),
            pltpu.VMEM((hp, seq, 2 * HEAD_DIM), BF16),
            pltpu.VMEM((hp, HEAD_DIM + MOBA_SUM_ROWS, seq), BF16),
            pltpu.VMEM((hp, n_blk, blk), F32),
            pltpu.VMEM((blk, blk), F32),
        ],
        compiler_params=_params(("parallel", "parallel", "arbitrary")),
        name="moba",
    )(qkv, qkv, qkv)


def _mix_kernel(c_ref, a_ref, gates_ref, x_ref, wpw_ref, wao_ref, wout_ref, g2_ref, wrt_ref, br_ref,
                h_ref, route_ref, cnt_ref, carry_ref):
    i = pl.program_id(0)
    tm, d = x_ref.shape

    @pl.when(i == 0)
    def _():
        carry_ref[...] = jnp.zeros_like(carry_ref)

    yc = jnp.dot(c_ref[...], wpw_ref[...], preferred_element_type=F32)
    ya = jnp.dot(a_ref[...], wao_ref[...], preferred_element_type=F32)
    merged = gates_ref[:, :d].astype(F32) * yc + gates_ref[:, d:].astype(F32) * ya
    h = x_ref[...] + jnp.dot(merged.astype(BF16), wout_ref[...], preferred_element_type=F32)
    h_ref[...] = h
    hn = h * lax.rsqrt(jnp.mean(h * h, axis=-1, keepdims=True) + NORM_EPS) * g2_ref[...]
    logits = lax.dot_general(wrt_ref[...], hn.astype(BF16), NT_DIMS, preferred_element_type=F32) + br_ref[...]

    iota8 = lax.broadcasted_iota(jnp.int32, (N_GROUPS, tm), 0)
    gl = logits[0:N_GROUPS]
    gmax = jnp.max(gl, axis=0, keepdims=True)
    g_p = 1.0 / jnp.sum(jnp.exp(gl - gmax), axis=0, keepdims=True)
    g_idx = jnp.min(jnp.where(gl == gmax, iota8, N_GROUPS), axis=0, keepdims=True)
    esel = jnp.zeros((EXPERTS_PER_GROUP, tm), F32)
    for g in range(N_GROUPS):
        lo = N_GROUPS + g * EXPERTS_PER_GROUP
        esel = jnp.where(g_idx == g, logits[lo:lo + EXPERTS_PER_GROUP], esel)
    v1 = jnp.max(esel, axis=0, keepdims=True)
    i1 = jnp.min(jnp.where(esel == v1, iota8, EXPERTS_PER_GROUP), axis=0, keepdims=True)
    rest = jnp.where(iota8 == i1, -jnp.inf, esel)
    v2 = jnp.max(rest, axis=0, keepdims=True)
    i2 = jnp.min(jnp.where(rest == v2, iota8, EXPERTS_PER_GROUP), axis=0, keepdims=True)
    r = jnp.exp(v2 - v1)
    w1 = g_p / (1.0 + r)
    w2 = g_p * r / (1.0 + r)
    e1 = g_idx * EXPERTS_PER_GROUP + i1
    e2 = g_idx * EXPERTS_PER_GROUP + i2

    iota_e = lax.broadcasted_iota(jnp.int32, (N_EXPERTS, tm), 0)
    oh1 = iota_e == e1
    oh2 = iota_e == e2
    onehot = jnp.concatenate([jnp.where(oh1, 1.0, 0.0), jnp.where(oh2, 1.0, 0.0)], axis=1)
    before = (lax.broadcasted_iota(jnp.int32, (2 * tm, 2 * tm), 0)
              < lax.broadcasted_iota(jnp.int32, (2 * tm, 2 * tm), 1))
    prefix = jnp.dot(onehot.astype(BF16), jnp.where(before, 1.0, 0.0).astype(BF16), preferred_element_type=F32)
    carry = carry_ref[...]
    base = prefix + carry[:, 0:1]
    rank1 = jnp.sum(jnp.where(oh1, base[:, :tm], 0.0), axis=0, keepdims=True)
    rank2 = jnp.sum(jnp.where(oh2, base[:, tm:], 0.0), axis=0, keepdims=True)
    total = carry + jnp.sum(onehot, axis=1, keepdims=True)
    carry_ref[...] = total
    cnt_ref[...] = total

    route_ref[0:1, :] = e1.astype(F32)
    route_ref[1:2, :] = e2.astype(F32)
    route_ref[2:3, :] = w1
    route_ref[3:4, :] = w2
    route_ref[4:5, :] = rank1
    route_ref[5:6, :] = rank2
    route_ref[6:8, :] = jnp.zeros((2, tm), F32)


def _mix(c, a, gates, x, w_pw, w_ao, w_out, g2, wr_t, b_r):
    t, d = x.shape
    cw = c.shape[1]
    aw = a.shape[1]
    tm = _tile(t, 256)
    const = lambda shape: pl.BlockSpec(shape, lambda i: (0, 0), pipeline_mode=pl.Buffered(1))
    return pl.pallas_call(
        _mix_kernel,
        out_shape=(
            jax.ShapeDtypeStruct((t, d), F32),
            jax.ShapeDtypeStruct((8, t), F32),
            jax.ShapeDtypeStruct((N_EXPERTS, 128), F32),
        ),
        grid=(t // tm,),
        in_specs=[
            pl.BlockSpec((tm, cw), lambda i: (i, 0)),
            pl.BlockSpec((tm, aw), lambda i: (i, 0)),
            pl.BlockSpec((tm, 2 * d), lambda i: (i, 0)),
            pl.BlockSpec((tm, d), lambda i: (i, 0)),
            const((cw, d)),
            const((aw, d)),
            const((d, d)),
            const((1, d)),
            const((ROUTER_ROWS, d)),
            const((ROUTER_ROWS, 1)),
        ],
        out_specs=(
            pl.BlockSpec((tm, d), lambda i: (i, 0)),
            pl.BlockSpec((8, tm), lambda i: (0, i)),
            pl.BlockSpec((N_EXPERTS, 128), lambda i: (0, 0)),
        ),
        scratch_shapes=[pltpu.VMEM((N_EXPERTS, 128), F32)],
        compiler_params=_params(("arbitrary",)),
        name="mix",
    )(c, a, gates, x, w_pw, w_ao, w_out, g2.reshape(1, d), wr_t, b_r)


def _row_copy(src_hbm, row, dst_vmem, dst_row, sem):
    return pltpu.make_async_copy(src_hbm.at[pl.ds(row, 1), :], dst_vmem.at[pl.ds(dst_row, 1), :], sem)


def _dispatch_kernel(slot1_ref, slot2_ref, pad_lo_ref, pad_hi_ref, n_valid_ref, h_ref, xs_hbm, hbuf, sems,
                     *, n_blocks):
    i = pl.program_id(0)
    n_steps = pl.num_programs(0)
    tm = h_ref.shape[0]
    buf = i % 2
    hbuf[buf] = h_ref[...]

    def token_copies(step, b):
        cps = []
        for r in range(tm):
            t = step * tm + r
            src = hbuf.at[b, pl.ds(r, 1), :]
            cps.append(pltpu.make_async_copy(src, xs_hbm.at[pl.ds(slot1_ref[t], 1), :], sems.at[b]))
            cps.append(pltpu.make_async_copy(src, xs_hbm.at[pl.ds(slot2_ref[t], 1), :], sems.at[b]))
        return cps

    def pad_copy(slot):
        return pltpu.make_async_copy(hbuf.at[buf, pl.ds(0, 1), :], xs_hbm.at[pl.ds(slot, 1), :], sems.at[2])

    def tail_copy(block):
        return pltpu.make_async_copy(hbuf.at[buf], xs_hbm.at[pl.ds(block * tm, tm), :], sems.at[3])

    for k, cp in enumerate(token_copies(i, buf)):
        cp.start(priority=k % 2)

    @pl.when(i > 0)
    def _():
        for cp in token_copies(i - 1, 1 - buf):
            cp.wait()

    @pl.when(i == n_steps - 1)
    def _():
        def start_tail(b, carry):
            tail_copy(b).start()
            return carry
        lax.fori_loop(n_valid_ref[0], n_blocks, start_tail, 0)
        for e in range(N_EXPERTS):
            def start_pad(s, carry):
                pad_copy(s).start()
                return carry
            lax.fori_loop(pad_lo_ref[e], pad_hi_ref[e], start_pad, 0)
        for cp in token_copies(i, buf):
            cp.wait()
        for e in range(N_EXPERTS):
            def wait_pad(s, carry):
                pad_copy(s).wait()
                return carry
            lax.fori_loop(pad_lo_ref[e], pad_hi_ref[e], wait_pad, 0)

        def wait_tail(b, carry):
            tail_copy(b).wait()
            return carry
        lax.fori_loop(n_valid_ref[0], n_blocks, wait_tail, 0)


def _dispatch(h, slot1, slot2, pad_lo, pad_hi, n_valid, n_blocks):
    t, d = h.shape
    tm = MOE_BLOCK
    assert t % tm == 0
    grid_spec = pltpu.PrefetchScalarGridSpec(
        num_scalar_prefetch=5,
        grid=(t // tm,),
        in_specs=[pl.BlockSpec((tm, d), lambda i, *_: (i, 0))],
        out_specs=pl.BlockSpec(memory_space=pl.ANY),
        scratch_shapes=[pltpu.VMEM((2, tm, d), h.dtype), pltpu.SemaphoreType.DMA((4,))],
    )
    return pl.pallas_call(
        functools.partial(_dispatch_kernel, n_blocks=n_blocks),
        out_shape=jax.ShapeDtypeStruct((n_blocks * MOE_BLOCK, d), h.dtype),
        grid_spec=grid_spec,
        compiler_params=_params(("arbitrary",)),
        name="dispatch",
    )(slot1, slot2, pad_lo, pad_hi, n_valid, h)


def _experts_kernel(blk_expert_ref, n_valid_ref, next_expert_ref, parity_ref,
                    x_ref, g2_ref, wg_hbm, wu_hbm, wd_hbm, y_ref,
                    wg_f, wu_f, wd_f, wsem, wg_b, wu_b, wd_b):
    n = pl.program_id(0)
    n_valid = n_valid_ref[0]

    def weight_copies(e, slot):
        return [pltpu.make_async_copy(wg_hbm.at[e], wg_f.at[slot], wsem.at[slot]),
                pltpu.make_async_copy(wu_hbm.at[e], wu_f.at[slot], wsem.at[slot]),
                pltpu.make_async_copy(wd_hbm.at[e], wd_f.at[slot], wsem.at[slot])]

    def start_weights(e, slot):
        for cp in weight_copies(e, slot):
            cp.start()

    @pl.when(n == 0)
    def _():
        start_weights(blk_expert_ref[0], 0)

    @pl.when(n >= n_valid)
    def _():
        y_ref[...] = jnp.zeros_like(y_ref)

    @pl.when(n < n_valid)
    def _():
        e = blk_expert_ref[n]
        wslot = parity_ref[e]

        @pl.when(jnp.logical_or(n == 0, blk_expert_ref[jnp.maximum(n - 1, 0)] != e))
        def _():
            nxt = next_expert_ref[e]

            @pl.when(nxt < N_EXPERTS)
            def _():
                start_weights(nxt, 1 - wslot)

            for cp in weight_copies(e, wslot):
                cp.wait()
            wg_b[...] = wg_f[wslot].astype(BF16)
            wu_b[...] = wu_f[wslot].astype(BF16)
            wd_b[...] = wd_f[wslot].astype(BF16)

        x = x_ref[...]
        xn = x * lax.rsqrt(jnp.mean(x * x, axis=-1, keepdims=True) + NORM_EPS) * g2_ref[...]
        xb = xn.astype(BF16)
        gate = jnp.dot(xb, wg_b[...], preferred_element_type=F32)
        up = jnp.dot(xb, wu_b[...], preferred_element_type=F32)
        hidden = (gate * jax.nn.sigmoid(gate) * up).astype(BF16)
        y_ref[...] = jnp.dot(hidden, wd_b[...], preferred_element_type=F32)


def _experts(xs, g2, w_g, w_u, w_d, blk_expert, n_valid, next_expert, parity):
    d = xs.shape[1]
    de = w_g.shape[2]
    n_blocks = blk_expert.shape[0]
    rows = MOE_BLOCK
    grid_spec = pltpu.PrefetchScalarGridSpec(
        num_scalar_prefetch=4,
        grid=(n_blocks,),
        in_specs=[
            pl.BlockSpec((rows, d), lambda n, be, nv, *_: (jnp.minimum(n, nv[0] - 1), 0)),
            pl.BlockSpec((1, d), lambda n, *_: (0, 0)),
            pl.BlockSpec(memory_space=pl.ANY),
            pl.BlockSpec(memory_space=pl.ANY),
            pl.BlockSpec(memory_space=pl.ANY),
        ],
        out_specs=pl.BlockSpec((rows, d), lambda n, *_: (n, 0)),
        scratch_shapes=[
            pltpu.VMEM((2, d, de), F32),
            pltpu.VMEM((2, d, de), F32),
            pltpu.VMEM((2, de, d), F32),
            pltpu.SemaphoreType.DMA((2,)),
            pltpu.VMEM((d, de), BF16),
            pltpu.VMEM((d, de), BF16),
            pltpu.VMEM((de, d), BF16),
        ],
    )
    return pl.pallas_call(
        _experts_kernel,
        out_shape=jax.ShapeDtypeStruct((n_blocks * rows, d), F32),
        grid_spec=grid_spec,
        compiler_params=_params(("arbitrary",)),
        name="experts",
    )(blk_expert, n_valid, next_expert, parity, xs, g2.reshape(1, d), w_g, w_u, w_d)


def _combine_kernel(slot1_ref, slot2_ref, h_ref, w_ref, gf_ref, y_hbm, o_ref, ybuf, sems):
    i = pl.program_id(0)
    n_tiles = pl.num_programs(0)
    tm = h_ref.shape[0]

    def gather(tile, buf):
        cps = []
        for r in range(tm):
            cps.append(_row_copy(y_hbm, slot1_ref[tile * tm + r], ybuf.at[buf, 0], r, sems.at[buf]))
            cps.append(_row_copy(y_hbm, slot2_ref[tile * tm + r], ybuf.at[buf, 1], r, sems.at[buf]))
        return cps

    def start_all(copies):
        for k, cp in enumerate(copies):
            cp.start(priority=k % 2)

    @pl.when(i == 0)
    def _():
        start_all(gather(0, 0))

    @pl.when(i + 1 < n_tiles)
    def _():
        start_all(gather(i + 1, (i + 1) % 2))

    buf = i % 2
    for cp in gather(i, buf):
        cp.wait()
    w = w_ref[...]
    h = h_ref[...] + w[:, 0:1] * ybuf[buf, 0] + w[:, 1:2] * ybuf[buf, 1]
    o_ref[...] = h * lax.rsqrt(jnp.mean(h * h, axis=-1, keepdims=True) + NORM_EPS) * gf_ref[...]


def _combine(h, y, slot1, slot2, w, gf):
    t, d = h.shape
    tm = _tile(t, 128)
    grid_spec = pltpu.PrefetchScalarGridSpec(
        num_scalar_prefetch=2,
        grid=(t // tm,),
        in_specs=[
            pl.BlockSpec((tm, d), lambda i, s1, s2: (i, 0)),
            pl.BlockSpec((tm, 2), lambda i, s1, s2: (i, 0)),
            pl.BlockSpec((1, d), lambda i, s1, s2: (0, 0)),
            pl.BlockSpec(memory_space=pl.ANY),
        ],
        out_specs=pl.BlockSpec((tm, d), lambda i, s1, s2: (i, 0)),
        scratch_shapes=[pltpu.VMEM((2, 2, tm, d), F32), pltpu.SemaphoreType.DMA((2,))],
    )
    return pl.pallas_call(
        _combine_kernel,
        out_shape=jax.ShapeDtypeStruct((t, d), F32),
        grid_spec=grid_spec,
        compiler_params=_params(("arbitrary",)),
        name="combine",
    )(slot1, slot2, h, w, gf.reshape(1, d), y)


def _layer(h, batch, seq, norm1_g, w_in, conv_dw_w, conv_dw_b, conv_ln_g, conv_ln_b, w_conv_out, w_attn_out,
           gate_b, w_out, norm2_g, w_rg, b_rg, w_re, b_re, w_exp_gate, w_exp_up, w_exp_down):
    t, d = h.shape
    c = w_conv_out.shape[0]
    aw = w_attn_out.shape[0]
    n_heads = aw // HEAD_DIM
    c1 = 2 * c
    c2 = c1 + 3 * aw

    w_in_b = w_in.astype(BF16)
    z, u = _norm_glu_proj(h, norm1_g, w_in_b[:, :c], w_in_b[:, c:c1])
    qkv = _proj_call(_proj_kernel, u, [w_in_b[:, c1:c2]], [], 3 * aw, BF16, "qkv_proj")
    gates = _proj_call(_gate_proj_kernel, u, [w_in_b[:, c2:]], [gate_b.reshape(1, 2 * d)], 2 * d, BF16, "gate_proj")

    conv = _conv_branch(z, seq, conv_dw_w, conv_dw_b, conv_ln_g, conv_ln_b)
    attn = _moba(qkv, batch, seq, n_heads)

    wr_t = jnp.zeros((ROUTER_ROWS, d), F32).at[:N_GROUPS].set(w_rg.T).at[N_GROUPS:N_GROUPS + N_EXPERTS].set(w_re.T)
    b_r = jnp.zeros((ROUTER_ROWS,), F32).at[:N_GROUPS].set(b_rg).at[N_GROUPS:N_GROUPS + N_EXPERTS].set(b_re)
    h_mid, route, counts = _mix(conv, attn, gates, h, w_conv_out.astype(BF16), w_attn_out.astype(BF16),
                                w_out.astype(BF16), norm2_g, wr_t.astype(BF16), b_r.reshape(ROUTER_ROWS, 1))

    n_assign = 2 * t
    n_blocks = -(-n_assign // MOE_BLOCK) + N_EXPERTS
    counts = counts[:, 0].astype(jnp.int32)
    padded = (counts + MOE_BLOCK - 1) // MOE_BLOCK * MOE_BLOCK
    pad_end = jnp.cumsum(padded)
    pad_start = pad_end - padded
    ids = jnp.arange(N_EXPERTS, dtype=jnp.int32)

    def slot_of(expert, rank):
        start = jnp.sum(jnp.where(ids[:, None] == expert[None, :].astype(jnp.int32), pad_start[:, None], 0), axis=0)
        return start + rank.astype(jnp.int32)

    slot1 = slot_of(route[0], route[4])
    slot2 = slot_of(route[1], route[5])
    blk_start = jnp.arange(n_blocks, dtype=jnp.int32) * MOE_BLOCK
    blk_expert = jnp.minimum(jnp.sum(pad_end[None, :] <= blk_start[:, None], axis=1), N_EXPERTS - 1).astype(jnp.int32)
    n_valid = (pad_end[-1:] // MOE_BLOCK).astype(jnp.int32)
    nonempty = counts > 0
    later = jnp.logical_and(ids[None, :] > ids[:, None], nonempty[None, :])
    next_expert = jnp.min(jnp.where(later, ids[None, :], N_EXPERTS), axis=1).astype(jnp.int32)
    parity = (jnp.sum(jnp.logical_and(ids[None, :] < ids[:, None], nonempty[None, :]), axis=1) % 2).astype(jnp.int32)

    xs = _dispatch(h_mid, slot1, slot2, pad_start + counts, pad_end, n_valid, n_blocks)
    y = _experts(xs, norm2_g, w_exp_gate, w_exp_up, w_exp_down, blk_expert, n_valid, next_expert, parity)
    weights = jnp.stack([route[2], route[3]], axis=1)
    return h_mid, y, slot1, slot2, weights


def kernel(x, norm1_g, w_in, conv_dw_w, conv_dw_b, conv_ln_g, conv_ln_b, w_conv_out, w_attn_out, gate_b, w_out,
           norm2_g, w_router_group, b_router_group, w_router_expert, b_router_expert, w_exp_gate, w_exp_up,
           w_exp_down, norm_f_g):
    b, s, d = x.shape
    assert norm1_g.shape[0] == 1, "single-layer stack: the combine kernel fuses the final RMSNorm"
    h_mid, y, slot1, slot2, weights = _layer(
        x.reshape(b * s, d), b, s, norm1_g[0], w_in[0], conv_dw_w[0], conv_dw_b[0], conv_ln_g[0], conv_ln_b[0],
        w_conv_out[0], w_attn_out[0], gate_b[0], w_out[0], norm2_g[0], w_router_group[0],
        b_router_group[0], w_router_expert[0], b_router_expert[0], w_exp_gate[0], w_exp_up[0], w_exp_down[0])
    return _combine(h_mid, y, slot1, slot2, weights, norm_f_g).reshape(b, s, d)
```

```python
import functools

import jax
import jax.numpy as jnp
from jax import lax
from jax.experimental import pallas as pl
from jax.experimental.pallas import tpu as pltpu

F32 = jnp.float32
BF16 = jnp.bfloat16

NORM_EPS = 1e-6
NEG_BIG = -1e30
MASKED = -2e30
LOG2_E = 1.4426950408889634
MOBA_GROUP = 4
MOBA_HEADS_PER_STEP = 4
MOBA_BIAS_PIECES = 3
MOBA_SUM_ROWS = 8
CONV_TAPS = 31
HEAD_DIM = 128
MOBA_BLOCK = 256
MOBA_TOPK = 3
N_GROUPS = 8
EXPERTS_PER_GROUP = 8
N_EXPERTS = N_GROUPS * EXPERTS_PER_GROUP
MOE_BLOCK = 256
ROUTER_ROWS = 128
CONV_HALO = 32
SUBLANES = 8
V7X_VMEM_LIMIT_BYTES = 56 * 1024 * 1024

NT_DIMS = (((1,), (1,)), ((), ()))


def _params(semantics, flags=None):
    return pltpu.CompilerParams(dimension_semantics=semantics, vmem_limit_bytes=V7X_VMEM_LIMIT_BYTES, flags=flags)


def _tile(n, want):
    t = min(n, want)
    while n % t:
        t //= 2
    return t


def _norm_glu_proj_kernel(x_ref, g_ref, wv_ref, wg_ref, z_ref, u_ref):
    x = x_ref[...]
    u = (x * lax.rsqrt(jnp.mean(x * x, axis=-1, keepdims=True) + NORM_EPS) * g_ref[...]).astype(u_ref.dtype)
    u_ref[...] = u
    val = jnp.dot(u, wv_ref[...], preferred_element_type=F32)
    gate = jnp.dot(u, wg_ref[...], preferred_element_type=F32)
    z_ref[...] = val * jax.nn.sigmoid(gate)


def _norm_glu_proj(x, g, w_val, w_gate):
    t, d = x.shape
    c = w_val.shape[1]
    tm = _tile(t, 512)
    const = lambda shape: pl.BlockSpec(shape, lambda i: (0, 0), pipeline_mode=pl.Buffered(1))
    return pl.pallas_call(
        _norm_glu_proj_kernel,
        out_shape=(jax.ShapeDtypeStruct((t, c), F32), jax.ShapeDtypeStruct((t, d), BF16)),
        grid=(t // tm,),
        in_specs=[pl.BlockSpec((tm, d), lambda i: (i, 0)), const((1, d)), const((d, c)), const((d, c))],
        out_specs=(pl.BlockSpec((tm, c), lambda i: (i, 0)), pl.BlockSpec((tm, d), lambda i: (i, 0))),
        compiler_params=_params(("parallel",)),
        name="norm_glu_proj",
    )(x, g.reshape(1, d), w_val, w_gate)


def _proj_kernel(u_ref, w_ref, o_ref):
    o_ref[...] = jnp.dot(u_ref[...], w_ref[...], preferred_element_type=F32).astype(o_ref.dtype)


def _gate_proj_kernel(u_ref, w_ref, b_ref, o_ref):
    y = jnp.dot(u_ref[...], w_ref[...], preferred_element_type=F32) + b_ref[...]
    o_ref[...] = jax.nn.sigmoid(y).astype(o_ref.dtype)


def _proj_call(kernel, u, weights, extra, n_out, out_dtype, name):
    t, k = u.shape
    tm = _tile(t, 1024)
    tn = _tile(n_out, 1024)
    in_specs = [pl.BlockSpec((tm, k), lambda i, j: (i, 0))]
    in_specs += [pl.BlockSpec((k, tn), lambda i, j: (0, j)) for _ in weights]
    in_specs += [pl.BlockSpec((1, tn), lambda i, j: (0, j)) for _ in extra]
    return pl.pallas_call(
        kernel,
        out_shape=jax.ShapeDtypeStruct((t, n_out), out_dtype),
        grid=(t // tm, n_out // tn),
        in_specs=in_specs,
        out_specs=pl.BlockSpec((tm, tn), lambda i, j: (i, j)),
        compiler_params=_params(("parallel", "parallel")),
        name=name,
    )(u, *weights, *extra)


def _conv_kernel(z_ref, halo_ref, w_ref, b_ref, g_ref, beta_ref, o_ref, ext_ref, acc_ref, *, tiles_per_seq):
    i = pl.program_id(0)
    ts, c = z_ref.shape
    seq_start = (i % tiles_per_seq) == 0
    n_lt = c // 128
    span = ts + CONV_HALO - SUBLANES
    for lt in range(n_lt):
        ls = slice(lt * 128, (lt + 1) * 128)
        ext_ref[0, lt, 0:CONV_HALO, :] = jnp.where(seq_start, 0.0, halo_ref[:, ls])
        ext_ref[0, lt, CONV_HALO:, :] = z_ref[:, ls]
        for s in range(1, SUBLANES):
            ext_ref[s, lt, 0:span, :] = ext_ref[0, lt, s:s + span, :]
    rows = 64
    first = CONV_HALO - (CONV_TAPS - 1)
    n_rc = ts // rows

    def chunk(ci, carry):
        lt = ci // n_rc
        r0 = pl.multiple_of((ci % n_rc) * rows, rows)
        accs = [jnp.broadcast_to(b_ref[lt], (rows, 128)), jnp.zeros((rows, 128), F32)]
        for tap in range(CONV_TAPS):
            s = (first + tap) % SUBLANES
            base = pl.multiple_of(r0 + (first + tap - s), SUBLANES)
            accs[tap % 2] = accs[tap % 2] + ext_ref[s, lt, pl.ds(base, rows), :] * w_ref[lt, tap:tap + 1, :]
        acc_ref[lt, pl.ds(r0, rows), :] = accs[0] + accs[1]
        return carry

    lax.fori_loop(0, n_lt * n_rc, chunk, 0)
    y = jnp.concatenate([acc_ref[lt] for lt in range(n_lt)], axis=1)
    mu = jnp.mean(y, axis=-1, keepdims=True)
    yc = y - mu
    var = jnp.mean(yc * yc, axis=-1, keepdims=True)
    yn = yc * lax.rsqrt(var + NORM_EPS) * g_ref[...] + beta_ref[...]
    o_ref[...] = (yn * jax.nn.sigmoid(yn)).astype(o_ref.dtype)


def _conv_branch(z, seq, w_dw, b_dw, ln_g, ln_b):
    t, c = z.shape
    ts = _tile(seq, 256)
    hb = ts // CONV_HALO
    kernel = functools.partial(_conv_kernel, tiles_per_seq=seq // ts)
    row = lambda v: v.reshape(1, c)
    n_lt = c // 128
    w_lt = w_dw.reshape(CONV_TAPS, n_lt, 128).transpose(1, 0, 2)
    b_lt = b_dw.reshape(n_lt, 1, 128)
    return pl.pallas_call(
        kernel,
        out_shape=jax.ShapeDtypeStruct((t, c), BF16),
        grid=(t // ts,),
        in_specs=[
            pl.BlockSpec((ts, c), lambda i: (i, 0)),
            pl.BlockSpec((CONV_HALO, c), lambda i: (jnp.maximum(i * hb - 1, 0), 0)),
            pl.BlockSpec((n_lt, CONV_TAPS, 128), lambda i: (0, 0, 0)),
            pl.BlockSpec((n_lt, 1, 128), lambda i: (0, 0, 0)),
            pl.BlockSpec((1, c), lambda i: (0, 0)),
            pl.BlockSpec((1, c), lambda i: (0, 0)),
        ],
        out_specs=pl.BlockSpec((ts, c), lambda i: (i, 0)),
        scratch_shapes=[pltpu.VMEM((SUBLANES, n_lt, ts + CONV_HALO, 128), F32), pltpu.VMEM((n_lt, ts, 128), F32)],
        compiler_params=_params(("parallel",)),
        name="conv",
    )(z, z, w_lt, b_lt, row(ln_g), row(ln_b))


def _moba_kernel(q_ref, k_ref, v_ref, o_ref, kmean_ref, kaug_ref, vt_ref, sel_ref, mask_ref,
                 *, n_blk, n_heads, hp, scale):
    head0 = pl.program_id(1) * hp
    qb = pl.program_id(2)
    blk = MOBA_BLOCK
    dh = HEAD_DIM
    lanes = [slice(h * dh, (h + 1) * dh) for h in range(hp)]
    slope2 = [jnp.exp2(-8.0 * (head0 + h + 1).astype(F32) / n_heads) * LOG2_E for h in range(hp)]

    @pl.when(qb == 0)
    def _():
        k_local = lax.broadcasted_iota(jnp.int32, (blk, dh), 0).astype(F32)
        col = lax.broadcasted_iota(jnp.int32, (blk, dh), 1)
        extras = []
        for h in range(hp):
            rest = slope2[h] * k_local
            extra = jnp.zeros((blk, dh), F32)
            for piece in range(MOBA_BIAS_PIECES):
                part = rest.astype(BF16).astype(F32)
                extra = jnp.where(col == piece, part, extra)
                rest = rest - part
            extras.append(extra.astype(BF16))

        def per_block(kb, carry):
            st = pl.multiple_of(kb * blk, blk)
            for h in range(hp):
                k = k_ref[pl.ds(st, blk), lanes[h]]
                kmean_ref[h, pl.ds(kb, 1), :] = jnp.mean(k.astype(F32), axis=0, keepdims=True)
                kaug_ref[h, pl.ds(st, blk), 0:dh] = k
                kaug_ref[h, pl.ds(st, blk), dh:2 * dh] = extras[h]
                vt_ref[h, 0:dh, pl.ds(st, blk)] = v_ref[pl.ds(st, blk), lanes[h]].astype(F32).T.astype(BF16)
            return carry
        lax.fori_loop(0, n_blk, per_block, 0)
        ones_row = lax.broadcasted_iota(jnp.int32, (MOBA_SUM_ROWS, vt_ref.shape[2]), 0) == 0
        for h in range(hp):
            vt_ref[h, dh:dh + MOBA_SUM_ROWS, :] = jnp.where(ones_row, 1.0, 0.0).astype(BF16)
        key_i = lax.broadcasted_iota(jnp.int32, (blk, blk), 0)
        query_i = lax.broadcasted_iota(jnp.int32, (blk, blk), 1)
        mask_ref[...] = jnp.where(query_i >= key_i, 0.0, MASKED)

    rows = lax.broadcasted_iota(jnp.int32, (n_blk, blk), 0)
    ones_cols = jnp.where(lax.broadcasted_iota(jnp.int32, (blk, dh), 1) < MOBA_BIAS_PIECES, 1.0, 0.0).astype(BF16)
    q_aug = []
    for h in range(hp):
        q = q_ref[:, lanes[h]]
        gate = lax.dot_general(kmean_ref[h].astype(BF16), q, NT_DIMS, preferred_element_type=F32)
        gate = jnp.where(rows < qb, gate, -jnp.inf)
        sel = jnp.zeros((n_blk, blk), F32)
        for _ in range(MOBA_TOPK):
            top = jnp.max(gate, axis=0, keepdims=True)
            idx = jnp.min(jnp.where(gate == top, rows, n_blk), axis=0, keepdims=True)
            pick = rows == idx
            sel = jnp.where(pick, jnp.where(top > -jnp.inf, 1.0, sel), sel)
            gate = jnp.where(pick, -jnp.inf, gate)
        sel_ref[h] = sel
        qs = (q.astype(F32) * (scale * LOG2_E)).astype(BF16)
        q_aug.append(jnp.concatenate([qs, ones_cols], axis=1))

    def block_scores(h, st):
        return lax.dot_general(kaug_ref[h, pl.ds(st, blk), :], q_aug[h], NT_DIMS, preferred_element_type=F32)

    own_st = pl.multiple_of(qb * blk, blk)
    own = [block_scores(h, own_st) + mask_ref[...] for h in range(hp)]
    carry = []
    for h in range(hp):
        m = jnp.max(own[h], axis=0, keepdims=True)
        p = jnp.exp2(own[h] - m).astype(BF16)
        carry.append((m, jnp.dot(vt_ref[h, :, pl.ds(own_st, blk)], p, preferred_element_type=F32)))

    def per_group(j, carry):
        starts = [pl.multiple_of((j * MOBA_GROUP + g) * blk, blk) for g in range(MOBA_GROUP)]
        chains = [(h, j * MOBA_GROUP + g, starts[g]) for h in range(hp) for g in range(MOBA_GROUP)]
        scores = [block_scores(h, st) for h, _, st in chains]
        stats = []
        for (h, kb, _), s in zip(chains, scores):
            mb = jnp.max(s, axis=0, keepdims=True)
            p = jnp.exp2(s - mb).astype(BF16)
            block_term = slope2[h] * ((kb - qb) * blk).astype(F32)
            mb = jnp.where(sel_ref[h, pl.ds(kb, 1), :] > 0.5, mb + block_term, MASKED)
            stats.append((mb, p))
        pvs = [jnp.dot(vt_ref[h, :, pl.ds(st, blk)], p, preferred_element_type=F32)
               for (h, _, st), (_, p) in zip(chains, stats)]
        out = []
        for h in range(hp):
            m, acc = carry[h]
            mine = [(mb, pv) for (hh, _, _), (mb, _), pv in zip(chains, stats, pvs) if hh == h]
            m_new = m
            for mb, _ in mine:
                m_new = jnp.maximum(m_new, mb)
            acc = jnp.exp2(m - m_new) * acc
            for mb, pv in mine:
                acc = acc + jnp.exp2(mb - m_new) * pv
            out.append((m_new, acc))
        return tuple(out)

    n_groups = (qb + MOBA_GROUP - 1) // MOBA_GROUP
    final = lax.fori_loop(0, n_groups, per_group, tuple(carry))
    for h in range(hp):
        _, acc = final[h]
        o_ref[:, lanes[h]] = (acc[0:dh] / acc[dh:dh + 1]).T.astype(o_ref.dtype)


def _moba(qkv, batch, seq, n_heads):
    t = qkv.shape[0]
    blk = MOBA_BLOCK
    n_blk = seq // blk
    hp = _tile(n_heads, MOBA_HEADS_PER_STEP)
    assert n_blk % MOBA_GROUP == 0
    kernel = functools.partial(_moba_kernel, n_blk=n_blk, n_heads=n_heads, hp=hp, scale=HEAD_DIM ** -0.5)
    wide = hp * HEAD_DIM
    hsteps = n_heads // hp
    return pl.pallas_call(
        kernel,
        out_shape=jax.ShapeDtypeStruct((t, n_heads * HEAD_DIM), BF16),
        grid=(batch, hsteps, n_blk),
        in_specs=[
            pl.BlockSpec((blk, wide), lambda b, h, i: (b * n_blk + i, h)),
            pl.BlockSpec((seq, wide), lambda b, h, i: (b, hsteps + h), pipeline_mode=pl.Buffered(1)),
            pl.BlockSpec((seq, wide), lambda b, h, i: (b, 2 * hsteps + h), pipeline_mode=pl.Buffered(1)),
        ],
        out_specs=pl.BlockSpec((blk, wide), lambda b, h, i: (b * n_blk + i, h)),
        scratch_shapes=[
            pltpu.VMEM((hp, n_blk, HEAD_DIM), F32),
            pltpu.VMEM((hp, seq, 2 * HEAD_DIM), BF16),
            pltpu.VMEM((hp, HEAD_DIM + MOBA_SUM_ROWS, seq), BF16),
            pltpu.VMEM((hp, n_blk, blk), F32),
            pltpu.VMEM((blk, blk), F32),
        ],
        compiler_params=_params(("parallel", "parallel", "arbitrary")),
        name="moba",
    )(qkv, qkv, qkv)


def _mix_kernel(c_ref, a_ref, gates_ref, x_ref, wpw_ref, wao_ref, wout_ref, g2_ref, wrt_ref, br_ref,
                h_ref, route_ref, cnt_ref, carry_ref):
    i = pl.program_id(0)
    tm, d = x_ref.shape

    @pl.when(i == 0)
    def _():
        carry_ref[...] = jnp.zeros_like(carry_ref)

    yc = jnp.dot(c_ref[...], wpw_ref[...], preferred_element_type=F32)
    ya = jnp.dot(a_ref[...], wao_ref[...], preferred_element_type=F32)
    merged = gates_ref[:, :d].astype(F32) * yc + gates_ref[:, d:].astype(F32) * ya
    h = x_ref[...] + jnp.dot(merged.astype(BF16), wout_ref[...], preferred_element_type=F32)
    h_ref[...] = h
    hn = h * lax.rsqrt(jnp.mean(h * h, axis=-1, keepdims=True) + NORM_EPS) * g2_ref[...]
    logits = lax.dot_general(wrt_ref[...], hn.astype(BF16), NT_DIMS, preferred_element_type=F32) + br_ref[...]

    iota8 = lax.broadcasted_iota(jnp.int32, (N_GROUPS, tm), 0)
    gl = logits[0:N_GROUPS]
    gmax = jnp.max(gl, axis=0, keepdims=True)
    g_p = 1.0 / jnp.sum(jnp.exp(gl - gmax), axis=0, keepdims=True)
    g_idx = jnp.min(jnp.where(gl == gmax, iota8, N_GROUPS), axis=0, keepdims=True)
    esel = jnp.zeros((EXPERTS_PER_GROUP, tm), F32)
    for g in range(N_GROUPS):
        lo = N_GROUPS + g * EXPERTS_PER_GROUP
        esel = jnp.where(g_idx == g, logits[lo:lo + EXPERTS_PER_GROUP], esel)
    v1 = jnp.max(esel, axis=0, keepdims=True)
    i1 = jnp.min(jnp.where(esel == v1, iota8, EXPERTS_PER_GROUP), axis=0, keepdims=True)
    rest = jnp.where(iota8 == i1, -jnp.inf, esel)
    v2 = jnp.max(rest, axis=0, keepdims=True)
    i2 = jnp.min(jnp.where(rest == v2, iota8, EXPERTS_PER_GROUP), axis=0, keepdims=True)
    r = jnp.exp(v2 - v1)
    w1 = g_p / (1.0 + r)
    w2 = g_p * r / (1.0 + r)
    e1 = g_idx * EXPERTS_PER_GROUP + i1
    e2 = g_idx * EXPERTS_PER_GROUP + i2

    iota_e = lax.broadcasted_iota(jnp.int32, (N_EXPERTS, tm), 0)
    oh1 = iota_e == e1
    oh2 = iota_e == e2
    onehot = jnp.concatenate([jnp.where(oh1, 1.0, 0.0), jnp.where(oh2, 1.0, 0.0)], axis=1)
    before = (lax.broadcasted_iota(jnp.int32, (2 * tm, 2 * tm), 0)
              < lax.broadcasted_iota(jnp.int32, (2 * tm, 2 * tm), 1))
    prefix = jnp.dot(onehot.astype(BF16), jnp.where(before, 1.0, 0.0).astype(BF16), preferred_element_type=F32)
    carry = carry_ref[...]
    base = prefix + carry[:, 0:1]
    rank1 = jnp.sum(jnp.where(oh1, base[:, :tm], 0.0), axis=0, keepdims=True)
    rank2 = jnp.sum(jnp.where(oh2, base[:, tm:], 0.0), axis=0, keepdims=True)
    total = carry + jnp.sum(onehot, axis=1, keepdims=True)
    carry_ref[...] = total
    cnt_ref[...] = total

    route_ref[0:1, :] = e1.astype(F32)
    route_ref[1:2, :] = e2.astype(F32)
    route_ref[2:3, :] = w1
    route_ref[3:4, :] = w2
    route_ref[4:5, :] = rank1
    route_ref[5:6, :] = rank2
    route_ref[6:8, :] = jnp.zeros((2, tm), F32)


def _mix(c, a, gates, x, w_pw, w_ao, w_out, g2, wr_t, b_r):
    t, d = x.shape
    cw = c.shape[1]
    aw = a.shape[1]
    tm = _tile(t, 256)
    const = lambda shape: pl.BlockSpec(shape, lambda i: (0, 0), pipeline_mode=pl.Buffered(1))
    return pl.pallas_call(
        _mix_kernel,
        out_shape=(
            jax.ShapeDtypeStruct((t, d), F32),
            jax.ShapeDtypeStruct((8, t), F32),
            jax.ShapeDtypeStruct((N_EXPERTS, 128), F32),
        ),
        grid=(t // tm,),
        in_specs=[
            pl.BlockSpec((tm, cw), lambda i: (i, 0)),
            pl.BlockSpec((tm, aw), lambda i: (i, 0)),
            pl.BlockSpec((tm, 2 * d), lambda i: (i, 0)),
            pl.BlockSpec((tm, d), lambda i: (i, 0)),
            const((cw, d)),
            const((aw, d)),
            const((d, d)),
            const((1, d)),
            const((ROUTER_ROWS, d)),
            const((ROUTER_ROWS, 1)),
        ],
        out_specs=(
            pl.BlockSpec((tm, d), lambda i: (i, 0)),
            pl.BlockSpec((8, tm), lambda i: (0, i)),
            pl.BlockSpec((N_EXPERTS, 128), lambda i: (0, 0)),
        ),
        scratch_shapes=[pltpu.VMEM((N_EXPERTS, 128), F32)],
        compiler_params=_params(("arbitrary",)),
        name="mix",
    )(c, a, gates, x, w_pw, w_ao, w_out, g2.reshape(1, d), wr_t, b_r)


def _row_copy(src_hbm, row, dst_vmem, dst_row, sem):
    return pltpu.make_async_copy(src_hbm.at[pl.ds(row, 1), :], dst_vmem.at[pl.ds(dst_row, 1), :], sem)


def _dispatch_kernel(slot1_ref, slot2_ref, pad_lo_ref, pad_hi_ref, n_valid_ref, h_ref, xs_hbm, hbuf, sems,
                     *, n_blocks):
    i = pl.program_id(0)
    n_steps = pl.num_programs(0)
    tm = h_ref.shape[0]
    buf = i % 2
    hbuf[buf] = h_ref[...]

    def token_copies(step, b):
        cps = []
        for r in range(tm):
            t = step * tm + r
            src = hbuf.at[b, pl.ds(r, 1), :]
            cps.append(pltpu.make_async_copy(src, xs_hbm.at[pl.ds(slot1_ref[t], 1), :], sems.at[b]))
            cps.append(pltpu.make_async_copy(src, xs_hbm.at[pl.ds(slot2_ref[t], 1), :], sems.at[b]))
        return cps

    def pad_copies(e, todo):
        lo = pad_lo_ref[e]
        aligned = (lo + SUBLANES - 1) // SUBLANES * SUBLANES

        def single(s, carry):
            todo(pltpu.make_async_copy(hbuf.at[buf, pl.ds(0, 1), :], xs_hbm.at[pl.ds(s, 1), :], sems.at[2]))
            return carry
        lax.fori_loop(lo, aligned, single, 0)
        n = pad_hi_ref[e] - aligned
        size = tm // 2
        while size >= SUBLANES:
            @pl.when((n & size) != 0)
            def _(size=size):
                first = pl.multiple_of(aligned + (n & ~(2 * size - 1)), SUBLANES)
                todo(pltpu.make_async_copy(hbuf.at[buf, pl.ds(0, size), :], xs_hbm.at[pl.ds(first, size), :],
                                           sems.at[2]))
            size //= 2

    def tail_copy(block):
        return pltpu.make_async_copy(hbuf.at[buf], xs_hbm.at[pl.ds(block * tm, tm), :], sems.at[3])

    for k, cp in enumerate(token_copies(i, buf)):
        cp.start(priority=k % 2)

    @pl.when(i > 0)
    def _():
        for cp in token_copies(i - 1, 1 - buf):
            cp.wait()

    @pl.when(i == n_steps - 1)
    def _():
        def start_tail(b, carry):
            tail_copy(b).start()
            return carry
        lax.fori_loop(n_valid_ref[0], n_blocks, start_tail, 0)
        for e in range(N_EXPERTS):
            pad_copies(e, lambda cp: cp.start())
        for cp in token_copies(i, buf):
            cp.wait()
        for e in range(N_EXPERTS):
            pad_copies(e, lambda cp: cp.wait())

        def wait_tail(b, carry):
            tail_copy(b).wait()
            return carry
        lax.fori_loop(n_valid_ref[0], n_blocks, wait_tail, 0)


def _dispatch(h, slot1, slot2, pad_lo, pad_hi, n_valid, n_blocks):
    t, d = h.shape
    tm = MOE_BLOCK
    assert t % tm == 0
    grid_spec = pltpu.PrefetchScalarGridSpec(
        num_scalar_prefetch=5,
        grid=(t // tm,),
        in_specs=[pl.BlockSpec((tm, d), lambda i, *_: (i, 0))],
        out_specs=pl.BlockSpec(memory_space=pl.ANY),
        scratch_shapes=[pltpu.VMEM((2, tm, d), h.dtype), pltpu.SemaphoreType.DMA((4,))],
    )
    return pl.pallas_call(
        functools.partial(_dispatch_kernel, n_blocks=n_blocks),
        out_shape=jax.ShapeDtypeStruct((n_blocks * MOE_BLOCK, d), h.dtype),
        grid_spec=grid_spec,
        compiler_params=_params(("arbitrary",)),
        name="dispatch",
    )(slot1, slot2, pad_lo, pad_hi, n_valid, h)


def _experts_kernel(blk_expert_ref, n_valid_ref, next_expert_ref, parity_ref,
                    x_ref, g2_ref, wg_hbm, wu_hbm, wd_hbm, y_ref,
                    wg_f, wu_f, wd_f, wsem, wg_b, wu_b, wd_b):
    n = pl.program_id(0)
    n_valid = n_valid_ref[0]

    def weight_copies(e, slot):
        return [pltpu.make_async_copy(wg_hbm.at[e], wg_f.at[slot], wsem.at[slot]),
                pltpu.make_async_copy(wu_hbm.at[e], wu_f.at[slot], wsem.at[slot]),
                pltpu.make_async_copy(wd_hbm.at[e], wd_f.at[slot], wsem.at[slot])]

    def start_weights(e, slot):
        for cp in weight_copies(e, slot):
            cp.start()

    @pl.when(n == 0)
    def _():
        start_weights(blk_expert_ref[0], 0)

    @pl.when(n >= n_valid)
    def _():
        y_ref[...] = jnp.zeros_like(y_ref)

    @pl.when(n < n_valid)
    def _():
        e = blk_expert_ref[n]
        wslot = parity_ref[e]

        @pl.when(jnp.logical_or(n == 0, blk_expert_ref[jnp.maximum(n - 1, 0)] != e))
        def _():
            nxt = next_expert_ref[e]

            @pl.when(nxt < N_EXPERTS)
            def _():
                start_weights(nxt, 1 - wslot)

            for cp in weight_copies(e, wslot):
                cp.wait()
            wg_b[...] = wg_f[wslot].astype(BF16)
            wu_b[...] = wu_f[wslot].astype(BF16)
            wd_b[...] = wd_f[wslot].astype(BF16)

        x = x_ref[...]
        xn = x * lax.rsqrt(jnp.mean(x * x, axis=-1, keepdims=True) + NORM_EPS) * g2_ref[...]
        xb = xn.astype(BF16)
        gate = jnp.dot(xb, wg_b[...], preferred_element_type=F32)
        up = jnp.dot(xb, wu_b[...], preferred_element_type=F32)
        hidden = (gate * jax.nn.sigmoid(gate) * up).astype(BF16)
        y_ref[...] = jnp.dot(hidden, wd_b[...], preferred_element_type=F32)


def _experts(xs, g2, w_g, w_u, w_d, blk_expert, n_valid, next_expert, parity):
    d = xs.shape[1]
    de = w_g.shape[2]
    n_blocks = blk_expert.shape[0]
    rows = MOE_BLOCK
    grid_spec = pltpu.PrefetchScalarGridSpec(
        num_scalar_prefetch=4,
        grid=(n_blocks,),
        in_specs=[
            pl.BlockSpec((rows, d), lambda n, be, nv, *_: (jnp.minimum(n, nv[0] - 1), 0)),
            pl.BlockSpec((1, d), lambda n, *_: (0, 0)),
            pl.BlockSpec(memory_space=pl.ANY),
            pl.BlockSpec(memory_space=pl.ANY),
            pl.BlockSpec(memory_space=pl.ANY),
        ],
        out_specs=pl.BlockSpec((rows, d), lambda n, *_: (n, 0)),
        scratch_shapes=[
            pltpu.VMEM((2, d, de), F32),
            pltpu.VMEM((2, d, de), F32),
            pltpu.VMEM((2, de, d), F32),
            pltpu.SemaphoreType.DMA((2,)),
            pltpu.VMEM((d, de), BF16),
            pltpu.VMEM((d, de), BF16),
            pltpu.VMEM((de, d), BF16),
        ],
    )
    return pl.pallas_call(
        _experts_kernel,
        out_shape=jax.ShapeDtypeStruct((n_blocks * rows, d), F32),
        grid_spec=grid_spec,
        compiler_params=_params(("arbitrary",)),
        name="experts",
    )(blk_expert, n_valid, next_expert, parity, xs, g2.reshape(1, d), w_g, w_u, w_d)


def _combine_kernel(slot1_ref, slot2_ref, h_ref, w_ref, gf_ref, y_hbm, o_ref, ybuf, sems):
    i = pl.program_id(0)
    n_tiles = pl.num_programs(0)
    tm = h_ref.shape[0]

    def gather(tile, buf):
        cps = []
        for r in range(tm):
            cps.append(_row_copy(y_hbm, slot1_ref[tile * tm + r], ybuf.at[buf, 0], r, sems.at[buf]))
            cps.append(_row_copy(y_hbm, slot2_ref[tile * tm + r], ybuf.at[buf, 1], r, sems.at[buf]))
        return cps

    def start_all(copies):
        for k, cp in enumerate(copies):
            cp.start(priority=k % 2)

    @pl.when(i == 0)
    def _():
        start_all(gather(0, 0))

    @pl.when(i + 1 < n_tiles)
    def _():
        start_all(gather(i + 1, (i + 1) % 2))

    buf = i % 2
    for cp in gather(i, buf):
        cp.wait()
    w = w_ref[...]
    h = h_ref[...] + w[:, 0:1] * ybuf[buf, 0] + w[:, 1:2] * ybuf[buf, 1]
    o_ref[...] = h * lax.rsqrt(jnp.mean(h * h, axis=-1, keepdims=True) + NORM_EPS) * gf_ref[...]


def _combine(h, y, slot1, slot2, w, gf):
    t, d = h.shape
    tm = _tile(t, 128)
    grid_spec = pltpu.PrefetchScalarGridSpec(
        num_scalar_prefetch=2,
        grid=(t // tm,),
        in_specs=[
            pl.BlockSpec((tm, d), lambda i, s1, s2: (i, 0)),
            pl.BlockSpec((tm, 2), lambda i, s1, s2: (i, 0)),
            pl.BlockSpec((1, d), lambda i, s1, s2: (0, 0)),
            pl.BlockSpec(memory_space=pl.ANY),
        ],
        out_specs=pl.BlockSpec((tm, d), lambda i, s1, s2: (i, 0)),
        scratch_shapes=[pltpu.VMEM((2, 2, tm, d), F32), pltpu.SemaphoreType.DMA((2,))],
    )
    return pl.pallas_call(
        _combine_kernel,
        out_shape=jax.ShapeDtypeStruct((t, d), F32),
        grid_spec=grid_spec,
        compiler_params=_params(("arbitrary",)),
        name="combine",
    )(slot1, slot2, h, w, gf.reshape(1, d), y)


def _layer(h, batch, seq, norm1_g, w_in, conv_dw_w, conv_dw_b, conv_ln_g, conv_ln_b, w_conv_out, w_attn_out,
           gate_b, w_out, norm2_g, w_rg, b_rg, w_re, b_re, w_exp_gate, w_exp_up, w_exp_down):
    t, d = h.shape
    c = w_conv_out.shape[0]
    aw = w_attn_out.shape[0]
    n_heads = aw // HEAD_DIM
    c1 = 2 * c
    c2 = c1 + 3 * aw

    w_in_b = w_in.astype(BF16)
    z, u = _norm_glu_proj(h, norm1_g, w_in_b[:, :c], w_in_b[:, c:c1])
    qkv = _proj_call(_proj_kernel, u, [w_in_b[:, c1:c2]], [], 3 * aw, BF16, "qkv_proj")
    gates = _proj_call(_gate_proj_kernel, u, [w_in_b[:, c2:]], [gate_b.reshape(1, 2 * d)], 2 * d, BF16, "gate_proj")

    conv = _conv_branch(z, seq, conv_dw_w, conv_dw_b, conv_ln_g, conv_ln_b)
    attn = _moba(qkv, batch, seq, n_heads)

    wr_t = jnp.zeros((ROUTER_ROWS, d), F32).at[:N_GROUPS].set(w_rg.T).at[N_GROUPS:N_GROUPS + N_EXPERTS].set(w_re.T)
    b_r = jnp.zeros((ROUTER_ROWS,), F32).at[:N_GROUPS].set(b_rg).at[N_GROUPS:N_GROUPS + N_EXPERTS].set(b_re)
    h_mid, route, counts = _mix(conv, attn, gates, h, w_conv_out.astype(BF16), w_attn_out.astype(BF16),
                                w_out.astype(BF16), norm2_g, wr_t.astype(BF16), b_r.reshape(ROUTER_ROWS, 1))

    n_assign = 2 * t
    n_blocks = -(-n_assign // MOE_BLOCK) + N_EXPERTS
    counts = counts[:, 0].astype(jnp.int32)
    padded = (counts + MOE_BLOCK - 1) // MOE_BLOCK * MOE_BLOCK
    pad_end = jnp.cumsum(padded)
    pad_start = pad_end - padded
    ids = jnp.arange(N_EXPERTS, dtype=jnp.int32)

    def slot_of(expert, rank):
        start = jnp.sum(jnp.where(ids[:, None] == expert[None, :].astype(jnp.int32), pad_start[:, None], 0), axis=0)
        return start + rank.astype(jnp.int32)

    slot1 = slot_of(route[0], route[4])
    slot2 = slot_of(route[1], route[5])
    blk_start = jnp.arange(n_blocks, dtype=jnp.int32) * MOE_BLOCK
    blk_expert = jnp.minimum(jnp.sum(pad_end[None, :] <= blk_start[:, None], axis=1), N_EXPERTS - 1).astype(jnp.int32)
    n_valid = (pad_end[-1:] // MOE_BLOCK).astype(jnp.int32)
    nonempty = counts > 0
    later = jnp.logical_and(ids[None, :] > ids[:, None], nonempty[None, :])
    next_expert = jnp.min(jnp.where(later, ids[None, :], N_EXPERTS), axis=1).astype(jnp.int32)
    parity = (jnp.sum(jnp.logical_and(ids[None, :] < ids[:, None], nonempty[None, :]), axis=1) % 2).astype(jnp.int32)

    xs = _dispatch(h_mid, slot1, slot2, pad_start + counts, pad_end, n_valid, n_blocks)
    y = _experts(xs, norm2_g, w_exp_gate, w_exp_up, w_exp_down, blk_expert, n_valid, next_expert, parity)
    weights = jnp.stack([route[2], route[3]], axis=1)
    return h_mid, y, slot1, slot2, weights


def kernel(x, norm1_g, w_in, conv_dw_w, conv_dw_b, conv_ln_g, conv_ln_b, w_conv_out, w_attn_out, gate_b, w_out,
           norm2_g, w_router_group, b_router_group, w_router_expert, b_router_expert, w_exp_gate, w_exp_up,
           w_exp_down, norm_f_g):
    b, s, d = x.shape
    assert norm1_g.shape[0] == 1, "single-layer stack: the combine kernel fuses the final RMSNorm"
    h_mid, y, slot1, slot2, weights = _layer(
        x.reshape(b * s, d), b, s, norm1_g[0], w_in[0], conv_dw_w[0], conv_dw_b[0], conv_ln_g[0], conv_ln_b[0],
        w_conv_out[0], w_attn_out[0], gate_b[0], w_out[0], norm2_g[0], w_router_group[0],
        b_router_group[0], w_router_expert[0], b_router_expert[0], w_exp_gate[0], w_exp_up[0], w_exp_down[0])
    return _combine(h_mid, y, slot1, slot2, weights, norm_f_g).reshape(b, s, d)
```

```python
import functools
import math

import jax
import jax.numpy as jnp
from jax import lax
from jax.experimental import pallas as pl
from jax.experimental.pallas import tpu as pltpu

F32 = jnp.float32
BF16 = jnp.bfloat16

NORM_EPS = 1e-6
NEG_BIG = -1e30
MASKED = -2e30
LOG2_E = 1.4426950408889634
MOBA_GROUP = 4
MOBA_HEADS_PER_STEP = 4
MOBA_BIAS_PIECES = 3
MOBA_SUM_ROWS = 8
CONV_TAPS = 31
HEAD_DIM = 128
MOBA_BLOCK = 256
MOBA_TOPK = 3
N_GROUPS = 8
EXPERTS_PER_GROUP = 8
N_EXPERTS = N_GROUPS * EXPERTS_PER_GROUP
MOE_BLOCK = 256
ROUTER_ROWS = 128
CONV_HALO = 32
SUBLANES = 8
V7X_VMEM_LIMIT_BYTES = 56 * 1024 * 1024

NT_DIMS = (((1,), (1,)), ((), ()))


def _params(semantics, flags=None):
    return pltpu.CompilerParams(dimension_semantics=semantics, vmem_limit_bytes=V7X_VMEM_LIMIT_BYTES, flags=flags)


def _tile(n, want):
    t = min(n, want)
    while n % t:
        t //= 2
    return t


def _norm_glu_proj_kernel(x_ref, g_ref, wv_ref, wg_ref, z_ref, u_ref):
    x = x_ref[...]
    u = (x * lax.rsqrt(jnp.mean(x * x, axis=-1, keepdims=True) + NORM_EPS) * g_ref[...]).astype(u_ref.dtype)
    u_ref[...] = u
    val = jnp.dot(u, wv_ref[...], preferred_element_type=F32)
    gate = jnp.dot(u, wg_ref[...], preferred_element_type=F32)
    z_ref[...] = val * jax.nn.sigmoid(gate)


def _norm_glu_proj(x, g, w_in, c):
    t, d = x.shape
    tm = _tile(t, 512)
    const = lambda shape, col: pl.BlockSpec(shape, lambda i: (0, col), pipeline_mode=pl.Buffered(1))
    return pl.pallas_call(
        _norm_glu_proj_kernel,
        out_shape=(jax.ShapeDtypeStruct((t, c), F32), jax.ShapeDtypeStruct((t, d), BF16)),
        grid=(t // tm,),
        in_specs=[pl.BlockSpec((tm, d), lambda i: (i, 0)), const((1, d), 0), const((d, c), 0), const((d, c), 1)],
        out_specs=(pl.BlockSpec((tm, c), lambda i: (i, 0)), pl.BlockSpec((tm, d), lambda i: (i, 0))),
        compiler_params=_params(("parallel",)),
        name="norm_glu_proj",
    )(x, g.reshape(1, d), w_in, w_in)


def _proj_kernel(u_ref, w_ref, o_ref):
    o_ref[...] = jnp.dot(u_ref[...], w_ref[...], preferred_element_type=F32).astype(o_ref.dtype)


def _gate_proj_kernel(u_ref, w_ref, b_ref, o_ref):
    y = jnp.dot(u_ref[...], w_ref[...], preferred_element_type=F32) + b_ref[...]
    o_ref[...] = jax.nn.sigmoid(y).astype(o_ref.dtype)


def _proj_call(kernel, u, w_in, col0, extra, n_out, out_dtype, name):
    t, k = u.shape
    tm = _tile(t, 1024)
    tn = math.gcd(_tile(n_out, 1024), col0)
    first = col0 // tn
    in_specs = [pl.BlockSpec((tm, k), lambda i, j: (i, 0)),
                pl.BlockSpec((k, tn), lambda i, j: (0, first + j))]
    in_specs += [pl.BlockSpec((1, tn), lambda i, j: (0, j)) for _ in extra]
    return pl.pallas_call(
        kernel,
        out_shape=jax.ShapeDtypeStruct((t, n_out), out_dtype),
        grid=(t // tm, n_out // tn),
        in_specs=in_specs,
        out_specs=pl.BlockSpec((tm, tn), lambda i, j: (i, j)),
        compiler_params=_params(("parallel", "parallel")),
        name=name,
    )(u, w_in, *extra)


def _conv_kernel(z_ref, halo_ref, w_ref, b_ref, g_ref, beta_ref, o_ref, ext_ref, acc_ref, *, tiles_per_seq):
    i = pl.program_id(0)
    ts, c = z_ref.shape
    seq_start = (i % tiles_per_seq) == 0
    n_lt = c // 128
    span = ts + CONV_HALO - SUBLANES
    for lt in range(n_lt):
        ls = slice(lt * 128, (lt + 1) * 128)
        ext_ref[0, lt, 0:CONV_HALO, :] = jnp.where(seq_start, 0.0, halo_ref[:, ls])
        ext_ref[0, lt, CONV_HALO:, :] = z_ref[:, ls]
        for s in range(1, SUBLANES):
            ext_ref[s, lt, 0:span, :] = ext_ref[0, lt, s:s + span, :]
    rows = 64
    first = CONV_HALO - (CONV_TAPS - 1)
    n_rc = ts // rows

    def chunk(ci, carry):
        lt = ci // n_rc
        r0 = pl.multiple_of((ci % n_rc) * rows, rows)
        accs = [jnp.broadcast_to(b_ref[lt], (rows, 128)), jnp.zeros((rows, 128), F32)]
        for tap in range(CONV_TAPS):
            s = (first + tap) % SUBLANES
            base = pl.multiple_of(r0 + (first + tap - s), SUBLANES)
            accs[tap % 2] = accs[tap % 2] + ext_ref[s, lt, pl.ds(base, rows), :] * w_ref[lt, tap:tap + 1, :]
        acc_ref[lt, pl.ds(r0, rows), :] = accs[0] + accs[1]
        return carry

    lax.fori_loop(0, n_lt * n_rc, chunk, 0)
    y = jnp.concatenate([acc_ref[lt] for lt in range(n_lt)], axis=1)
    mu = jnp.mean(y, axis=-1, keepdims=True)
    yc = y - mu
    var = jnp.mean(yc * yc, axis=-1, keepdims=True)
    yn = yc * lax.rsqrt(var + NORM_EPS) * g_ref[...] + beta_ref[...]
    o_ref[...] = (yn * jax.nn.sigmoid(yn)).astype(o_ref.dtype)


def _conv_branch(z, seq, w_dw, b_dw, ln_g, ln_b):
    t, c = z.shape
    ts = _tile(seq, 256)
    hb = ts // CONV_HALO
    kernel = functools.partial(_conv_kernel, tiles_per_seq=seq // ts)
    row = lambda v: v.reshape(1, c)
    n_lt = c // 128
    w_lt = w_dw.reshape(CONV_TAPS, n_lt, 128).transpose(1, 0, 2)
    b_lt = b_dw.reshape(n_lt, 1, 128)
    return pl.pallas_call(
        kernel,
        out_shape=jax.ShapeDtypeStruct((t, c), BF16),
        grid=(t // ts,),
        in_specs=[
            pl.BlockSpec((ts, c), lambda i: (i, 0)),
            pl.BlockSpec((CONV_HALO, c), lambda i: (jnp.maximum(i * hb - 1, 0), 0)),
            pl.BlockSpec((n_lt, CONV_TAPS, 128), lambda i: (0, 0, 0)),
            pl.BlockSpec((n_lt, 1, 128), lambda i: (0, 0, 0)),
            pl.BlockSpec((1, c), lambda i: (0, 0)),
            pl.BlockSpec((1, c), lambda i: (0, 0)),
        ],
        out_specs=pl.BlockSpec((ts, c), lambda i: (i, 0)),
        scratch_shapes=[pltpu.VMEM((SUBLANES, n_lt, ts + CONV_HALO, 128), F32), pltpu.VMEM((n_lt, ts, 128), F32)],
        compiler_params=_params(("parallel",)),
        name="conv",
    )(z, z, w_lt, b_lt, row(ln_g), row(ln_b))


def _moba_kernel(q_ref, k_ref, v_ref, o_ref, kmean_ref, kaug_ref, vt_ref, sel_ref, mask_ref,
                 *, n_blk, n_heads, hp, scale):
    head0 = pl.program_id(1) * hp
    qb = pl.program_id(2)
    blk = MOBA_BLOCK
    dh = HEAD_DIM
    lanes = [slice(h * dh, (h + 1) * dh) for h in range(hp)]
    slope2 = [jnp.exp2(-8.0 * (head0 + h + 1).astype(F32) / n_heads) * LOG2_E for h in range(hp)]

    @pl.when(qb == 0)
    def _():
        k_local = lax.broadcasted_iota(jnp.int32, (blk, dh), 0).astype(F32)
        col = lax.broadcasted_iota(jnp.int32, (blk, dh), 1)
        extras = []
        for h in range(hp):
            rest = slope2[h] * k_local
            extra = jnp.zeros((blk, dh), F32)
            for piece in range(MOBA_BIAS_PIECES):
                part = rest.astype(BF16).astype(F32)
                extra = jnp.where(col == piece, part, extra)
                rest = rest - part
            extras.append(extra.astype(BF16))

        def per_block(kb, carry):
            st = pl.multiple_of(kb * blk, blk)
            for h in range(hp):
                k = k_ref[pl.ds(st, blk), lanes[h]]
                kmean_ref[h, pl.ds(kb, 1), :] = jnp.mean(k.astype(F32), axis=0, keepdims=True)
                kaug_ref[h, pl.ds(st, blk), 0:dh] = k
                kaug_ref[h, pl.ds(st, blk), dh:2 * dh] = extras[h]
                vt_ref[h, 0:dh, pl.ds(st, blk)] = v_ref[pl.ds(st, blk), lanes[h]].astype(F32).T.astype(BF16)
            return carry
        lax.fori_loop(0, n_blk, per_block, 0)
        ones_row = lax.broadcasted_iota(jnp.int32, (MOBA_SUM_ROWS, vt_ref.shape[2]), 0) == 0
        for h in range(hp):
            vt_ref[h, dh:dh + MOBA_SUM_ROWS, :] = jnp.where(ones_row, 1.0, 0.0).astype(BF16)
        key_i = lax.broadcasted_iota(jnp.int32, (blk, blk), 0)
        query_i = lax.broadcasted_iota(jnp.int32, (blk, blk), 1)
        mask_ref[...] = jnp.where(query_i >= key_i, 0.0, MASKED)

    rows = lax.broadcasted_iota(jnp.int32, (n_blk, blk), 0)
    ones_cols = jnp.where(lax.broadcasted_iota(jnp.int32, (blk, dh), 1) < MOBA_BIAS_PIECES, 1.0, 0.0).astype(BF16)
    q_aug = []
    for h in range(hp):
        q = q_ref[:, lanes[h]]
        gate = lax.dot_general(kmean_ref[h].astype(BF16), q, NT_DIMS, preferred_element_type=F32)
        gate = jnp.where(rows < qb, gate, -jnp.inf)
        sel = jnp.zeros((n_blk, blk), F32)
        for _ in range(MOBA_TOPK):
            top = jnp.max(gate, axis=0, keepdims=True)
            idx = jnp.min(jnp.where(gate == top, rows, n_blk), axis=0, keepdims=True)
            pick = rows == idx
            sel = jnp.where(pick, jnp.where(top > -jnp.inf, 1.0, sel), sel)
            gate = jnp.where(pick, -jnp.inf, gate)
        sel_ref[h] = sel
        qs = (q.astype(F32) * (scale * LOG2_E)).astype(BF16)
        q_aug.append(jnp.concatenate([qs, ones_cols], axis=1))

    def block_scores(h, st):
        return lax.dot_general(kaug_ref[h, pl.ds(st, blk), :], q_aug[h], NT_DIMS, preferred_element_type=F32)

    own_st = pl.multiple_of(qb * blk, blk)
    own = [block_scores(h, own_st) + mask_ref[...] for h in range(hp)]
    carry = []
    for h in range(hp):
        m = jnp.max(own[h], axis=0, keepdims=True)
        p = jnp.exp2(own[h] - m).astype(BF16)
        carry.append((m, jnp.dot(vt_ref[h, :, pl.ds(own_st, blk)], p, preferred_element_type=F32)))

    def per_group(j, carry):
        starts = [pl.multiple_of((j * MOBA_GROUP + g) * blk, blk) for g in range(MOBA_GROUP)]
        chains = [(h, j * MOBA_GROUP + g, starts[g]) for h in range(hp) for g in range(MOBA_GROUP)]
        scores = [block_scores(h, st) for h, _, st in chains]
        stats = []
        for (h, kb, _), s in zip(chains, scores):
            mb = jnp.max(s, axis=0, keepdims=True)
            p = jnp.exp2(s - mb).astype(BF16)
            block_term = slope2[h] * ((kb - qb) * blk).astype(F32)
            mb = jnp.where(sel_ref[h, pl.ds(kb, 1), :] > 0.5, mb + block_term, MASKED)
            stats.append((mb, p))
        pvs = [jnp.dot(vt_ref[h, :, pl.ds(st, blk)], p, preferred_element_type=F32)
               for (h, _, st), (_, p) in zip(chains, stats)]
        out = []
        for h in range(hp):
            m, acc = carry[h]
            mine = [(mb, pv) for (hh, _, _), (mb, _), pv in zip(chains, stats, pvs) if hh == h]
            m_new = m
            for mb, _ in mine:
                m_new = jnp.maximum(m_new, mb)
            acc = jnp.exp2(m - m_new) * acc
            for mb, pv in mine:
                acc = acc + jnp.exp2(mb - m_new) * pv
            out.append((m_new, acc))
        return tuple(out)

    n_groups = (qb + MOBA_GROUP - 1) // MOBA_GROUP
    final = lax.fori_loop(0, n_groups, per_group, tuple(carry))
    for h in range(hp):
        _, acc = final[h]
        o_ref[:, lanes[h]] = (acc[0:dh] / acc[dh:dh + 1]).T.astype(o_ref.dtype)


def _moba(qkv, batch, seq, n_heads):
    t = qkv.shape[0]
    blk = MOBA_BLOCK
    n_blk = seq // blk
    hp = _tile(n_heads, MOBA_HEADS_PER_STEP)
    assert n_blk % MOBA_GROUP == 0
    kernel = functools.partial(_moba_kernel, n_blk=n_blk, n_heads=n_heads, hp=hp, scale=HEAD_DIM ** -0.5)
    wide = hp * HEAD_DIM
    hsteps = n_heads // hp
    return pl.pallas_call(
        kernel,
        out_shape=jax.ShapeDtypeStruct((t, n_heads * HEAD_DIM), BF16),
        grid=(batch, hsteps, n_blk),
        in_specs=[
            pl.BlockSpec((blk, wide), lambda b, h, i: (b * n_blk + i, h)),
            pl.BlockSpec((seq, wide), lambda b, h, i: (b, hsteps + h), pipeline_mode=pl.Buffered(1)),
            pl.BlockSpec((seq, wide), lambda b, h, i: (b, 2 * hsteps + h), pipeline_mode=pl.Buffered(1)),
        ],
        out_specs=pl.BlockSpec((blk, wide), lambda b, h, i: (b * n_blk + i, h)),
        scratch_shapes=[
            pltpu.VMEM((hp, n_blk, HEAD_DIM), F32),
            pltpu.VMEM((hp, seq, 2 * HEAD_DIM), BF16),
            pltpu.VMEM((hp, HEAD_DIM + MOBA_SUM_ROWS, seq), BF16),
            pltpu.VMEM((hp, n_blk, blk), F32),
            pltpu.VMEM((blk, blk), F32),
        ],
        compiler_params=_params(("parallel", "parallel", "arbitrary")),
        name="moba",
    )(qkv, qkv, qkv)


def _mix_kernel(c_ref, a_ref, gates_ref, x_ref, wpw_ref, wao_ref, wout_ref, g2_ref, wrt_ref, br_ref,
                h_ref, route_ref, cnt_ref, carry_ref):
    i = pl.program_id(0)
    tm, d = x_ref.shape

    @pl.when(i == 0)
    def _():
        carry_ref[...] = jnp.zeros_like(carry_ref)

    yc = jnp.dot(c_ref[...], wpw_ref[...], preferred_element_type=F32)
    ya = jnp.dot(a_ref[...], wao_ref[...], preferred_element_type=F32)
    merged = gates_ref[:, :d].astype(F32) * yc + gates_ref[:, d:].astype(F32) * ya
    h = x_ref[...] + jnp.dot(merged.astype(BF16), wout_ref[...], preferred_element_type=F32)
    h_ref[...] = h
    hn = h * lax.rsqrt(jnp.mean(h * h, axis=-1, keepdims=True) + NORM_EPS) * g2_ref[...]
    logits = lax.dot_general(wrt_ref[...], hn.astype(BF16), NT_DIMS, preferred_element_type=F32) + br_ref[...]

    iota8 = lax.broadcasted_iota(jnp.int32, (N_GROUPS, tm), 0)
    gl = logits[0:N_GROUPS]
    gmax = jnp.max(gl, axis=0, keepdims=True)
    g_p = 1.0 / jnp.sum(jnp.exp(gl - gmax), axis=0, keepdims=True)
    g_idx = jnp.min(jnp.where(gl == gmax, iota8, N_GROUPS), axis=0, keepdims=True)
    esel = jnp.zeros((EXPERTS_PER_GROUP, tm), F32)
    for g in range(N_GROUPS):
        lo = N_GROUPS + g * EXPERTS_PER_GROUP
        esel = jnp.where(g_idx == g, logits[lo:lo + EXPERTS_PER_GROUP], esel)
    v1 = jnp.max(esel, axis=0, keepdims=True)
    i1 = jnp.min(jnp.where(esel == v1, iota8, EXPERTS_PER_GROUP), axis=0, keepdims=True)
    rest = jnp.where(iota8 == i1, -jnp.inf, esel)
    v2 = jnp.max(rest, axis=0, keepdims=True)
    i2 = jnp.min(jnp.where(rest == v2, iota8, EXPERTS_PER_GROUP), axis=0, keepdims=True)
    r = jnp.exp(v2 - v1)
    w1 = g_p / (1.0 + r)
    w2 = g_p * r / (1.0 + r)
    e1 = g_idx * EXPERTS_PER_GROUP + i1
    e2 = g_idx * EXPERTS_PER_GROUP + i2

    iota_e = lax.broadcasted_iota(jnp.int32, (N_EXPERTS, tm), 0)
    oh1 = iota_e == e1
    oh2 = iota_e == e2
    onehot = jnp.concatenate([jnp.where(oh1, 1.0, 0.0), jnp.where(oh2, 1.0, 0.0)], axis=1)
    before = (lax.broadcasted_iota(jnp.int32, (2 * tm, 2 * tm), 0)
              < lax.broadcasted_iota(jnp.int32, (2 * tm, 2 * tm), 1))
    prefix = jnp.dot(onehot.astype(BF16), jnp.where(before, 1.0, 0.0).astype(BF16), preferred_element_type=F32)
    carry = carry_ref[...]
    base = prefix + carry[:, 0:1]
    rank1 = jnp.sum(jnp.where(oh1, base[:, :tm], 0.0), axis=0, keepdims=True)
    rank2 = jnp.sum(jnp.where(oh2, base[:, tm:], 0.0), axis=0, keepdims=True)
    total = carry + jnp.sum(onehot, axis=1, keepdims=True)
    carry_ref[...] = total
    cnt_ref[...] = total

    route_ref[0:1, :] = e1.astype(F32)
    route_ref[1:2, :] = e2.astype(F32)
    route_ref[2:3, :] = w1
    route_ref[3:4, :] = w2
    route_ref[4:5, :] = rank1
    route_ref[5:6, :] = rank2
    route_ref[6:8, :] = jnp.zeros((2, tm), F32)


def _mix(c, a, gates, x, w_pw, w_ao, w_out, g2, wr_t, b_r):
    t, d = x.shape
    cw = c.shape[1]
    aw = a.shape[1]
    tm = _tile(t, 256)
    const = lambda shape: pl.BlockSpec(shape, lambda i: (0, 0), pipeline_mode=pl.Buffered(1))
    return pl.pallas_call(
        _mix_kernel,
        out_shape=(
            jax.ShapeDtypeStruct((t, d), F32),
            jax.ShapeDtypeStruct((8, t), F32),
            jax.ShapeDtypeStruct((N_EXPERTS, 128), F32),
        ),
        grid=(t // tm,),
        in_specs=[
            pl.BlockSpec((tm, cw), lambda i: (i, 0)),
            pl.BlockSpec((tm, aw), lambda i: (i, 0)),
            pl.BlockSpec((tm, 2 * d), lambda i: (i, 0)),
            pl.BlockSpec((tm, d), lambda i: (i, 0)),
            const((cw, d)),
            const((aw, d)),
            const((d, d)),
            const((1, d)),
            const((ROUTER_ROWS, d)),
            const((ROUTER_ROWS, 1)),
        ],
        out_specs=(
            pl.BlockSpec((tm, d), lambda i: (i, 0)),
            pl.BlockSpec((8, tm), lambda i: (0, i)),
            pl.BlockSpec((N_EXPERTS, 128), lambda i: (0, 0)),
        ),
        scratch_shapes=[pltpu.VMEM((N_EXPERTS, 128), F32)],
        compiler_params=_params(("arbitrary",)),
        name="mix",
    )(c, a, gates, x, w_pw, w_ao, w_out, g2.reshape(1, d), wr_t, b_r)


def _row_copy(src_hbm, row, dst_vmem, dst_row, sem):
    return pltpu.make_async_copy(src_hbm.at[pl.ds(row, 1), :], dst_vmem.at[pl.ds(dst_row, 1), :], sem)


def _dispatch_kernel(slot1_ref, slot2_ref, pad_lo_ref, pad_hi_ref, n_valid_ref, h_ref, xs_hbm, hbuf, sems,
                     *, n_blocks):
    i = pl.program_id(0)
    n_steps = pl.num_programs(0)
    tm = h_ref.shape[0]
    buf = i % 2
    hbuf[buf] = h_ref[...]

    def token_copies(step, b):
        cps = []
        for r in range(tm):
            t = step * tm + r
            src = hbuf.at[b, pl.ds(r, 1), :]
            cps.append(pltpu.make_async_copy(src, xs_hbm.at[pl.ds(slot1_ref[t], 1), :], sems.at[b]))
            cps.append(pltpu.make_async_copy(src, xs_hbm.at[pl.ds(slot2_ref[t], 1), :], sems.at[b]))
        return cps

    def pad_copies(e, todo):
        lo = pad_lo_ref[e]
        aligned = (lo + SUBLANES - 1) // SUBLANES * SUBLANES

        def single(s, carry):
            todo(pltpu.make_async_copy(hbuf.at[buf, pl.ds(0, 1), :], xs_hbm.at[pl.ds(s, 1), :], sems.at[2]))
            return carry
        lax.fori_loop(lo, aligned, single, 0)
        n = pad_hi_ref[e] - aligned
        size = tm // 2
        while size >= SUBLANES:
            @pl.when((n & size) != 0)
            def _(size=size):
                first = pl.multiple_of(aligned + (n & ~(2 * size - 1)), SUBLANES)
                todo(pltpu.make_async_copy(hbuf.at[buf, pl.ds(0, size), :], xs_hbm.at[pl.ds(first, size), :],
                                           sems.at[2]))
            size //= 2

    def tail_copy(block):
        return pltpu.make_async_copy(hbuf.at[buf], xs_hbm.at[pl.ds(block * tm, tm), :], sems.at[3])

    for k, cp in enumerate(token_copies(i, buf)):
        cp.start(priority=k % 2)

    @pl.when(i > 0)
    def _():
        for cp in token_copies(i - 1, 1 - buf):
            cp.wait()

    @pl.when(i == n_steps - 1)
    def _():
        def start_tail(b, carry):
            tail_copy(b).start()
            return carry
        lax.fori_loop(n_valid_ref[0], n_blocks, start_tail, 0)
        for e in range(N_EXPERTS):
            pad_copies(e, lambda cp: cp.start())
        for cp in token_copies(i, buf):
            cp.wait()
        for e in range(N_EXPERTS):
            pad_copies(e, lambda cp: cp.wait())

        def wait_tail(b, carry):
            tail_copy(b).wait()
            return carry
        lax.fori_loop(n_valid_ref[0], n_blocks, wait_tail, 0)


def _dispatch(h, slot1, slot2, pad_lo, pad_hi, n_valid, n_blocks):
    t, d = h.shape
    tm = MOE_BLOCK
    assert t % tm == 0
    grid_spec = pltpu.PrefetchScalarGridSpec(
        num_scalar_prefetch=5,
        grid=(t // tm,),
        in_specs=[pl.BlockSpec((tm, d), lambda i, *_: (i, 0))],
        out_specs=pl.BlockSpec(memory_space=pl.ANY),
        scratch_shapes=[pltpu.VMEM((2, tm, d), h.dtype), pltpu.SemaphoreType.DMA((4,))],
    )
    return pl.pallas_call(
        functools.partial(_dispatch_kernel, n_blocks=n_blocks),
        out_shape=jax.ShapeDtypeStruct((n_blocks * MOE_BLOCK, d), h.dtype),
        grid_spec=grid_spec,
        compiler_params=_params(("arbitrary",)),
        name="dispatch",
    )(slot1, slot2, pad_lo, pad_hi, n_valid, h)


def _experts_kernel(blk_expert_ref, n_valid_ref, next_expert_ref, parity_ref,
                    x_ref, g2_ref, wg_hbm, wu_hbm, wd_hbm, y_ref,
                    wg_f, wu_f, wd_f, wsem, wg_b, wu_b, wd_b):
    n = pl.program_id(0)
    n_valid = n_valid_ref[0]

    def weight_copies(e, slot):
        return [pltpu.make_async_copy(wg_hbm.at[e], wg_f.at[slot], wsem.at[slot]),
                pltpu.make_async_copy(wu_hbm.at[e], wu_f.at[slot], wsem.at[slot]),
                pltpu.make_async_copy(wd_hbm.at[e], wd_f.at[slot], wsem.at[slot])]

    def start_weights(e, slot):
        for cp in weight_copies(e, slot):
            cp.start()

    @pl.when(n == 0)
    def _():
        start_weights(blk_expert_ref[0], 0)

    @pl.when(n >= n_valid)
    def _():
        y_ref[...] = jnp.zeros_like(y_ref)

    @pl.when(n < n_valid)
    def _():
        e = blk_expert_ref[n]
        wslot = parity_ref[e]

        @pl.when(jnp.logical_or(n == 0, blk_expert_ref[jnp.maximum(n - 1, 0)] != e))
        def _():
            nxt = next_expert_ref[e]

            @pl.when(nxt < N_EXPERTS)
            def _():
                start_weights(nxt, 1 - wslot)

            for cp in weight_copies(e, wslot):
                cp.wait()
            wg_b[...] = wg_f[wslot].astype(BF16)
            wu_b[...] = wu_f[wslot].astype(BF16)
            wd_b[...] = wd_f[wslot].astype(BF16)

        x = x_ref[...]
        xn = x * lax.rsqrt(jnp.mean(x * x, axis=-1, keepdims=True) + NORM_EPS) * g2_ref[...]
        xb = xn.astype(BF16)
        gate = jnp.dot(xb, wg_b[...], preferred_element_type=F32)
        up = jnp.dot(xb, wu_b[...], preferred_element_type=F32)
        hidden = (gate * jax.nn.sigmoid(gate) * up).astype(BF16)
        y_ref[...] = jnp.dot(hidden, wd_b[...], preferred_element_type=F32)


def _experts(xs, g2, w_g, w_u, w_d, blk_expert, n_valid, next_expert, parity):
    d = xs.shape[1]
    de = w_g.shape[2]
    n_blocks = blk_expert.shape[0]
    rows = MOE_BLOCK
    grid_spec = pltpu.PrefetchScalarGridSpec(
        num_scalar_prefetch=4,
        grid=(n_blocks,),
        in_specs=[
            pl.BlockSpec((rows, d), lambda n, be, nv, *_: (jnp.minimum(n, nv[0] - 1), 0)),
            pl.BlockSpec((1, d), lambda n, *_: (0, 0)),
            pl.BlockSpec(memory_space=pl.ANY),
            pl.BlockSpec(memory_space=pl.ANY),
            pl.BlockSpec(memory_space=pl.ANY),
        ],
        out_specs=pl.BlockSpec((rows, d), lambda n, *_: (n, 0)),
        scratch_shapes=[
            pltpu.VMEM((2, d, de), F32),
            pltpu.VMEM((2, d, de), F32),
            pltpu.VMEM((2, de, d), F32),
            pltpu.SemaphoreType.DMA((2,)),
            pltpu.VMEM((d, de), BF16),
            pltpu.VMEM((d, de), BF16),
            pltpu.VMEM((de, d), BF16),
        ],
    )
    return pl.pallas_call(
        _experts_kernel,
        out_shape=jax.ShapeDtypeStruct((n_blocks * rows, d), F32),
        grid_spec=grid_spec,
        compiler_params=_params(("arbitrary",)),
        name="experts",
    )(blk_expert, n_valid, next_expert, parity, xs, g2.reshape(1, d), w_g, w_u, w_d)


def _combine_kernel(slot1_ref, slot2_ref, h_ref, w_ref, gf_ref, y_hbm, o_ref, ybuf, sems):
    i = pl.program_id(0)
    n_tiles = pl.num_programs(0)
    tm = h_ref.shape[0]

    def gather(tile, buf):
        cps = []
        for r in range(tm):
            cps.append(_row_copy(y_hbm, slot1_ref[tile * tm + r], ybuf.at[buf, 0], r, sems.at[buf]))
            cps.append(_row_copy(y_hbm, slot2_ref[tile * tm + r], ybuf.at[buf, 1], r, sems.at[buf]))
        return cps

    def start_all(copies):
        for k, cp in enumerate(copies):
            cp.start(priority=k % 2)

    @pl.when(i == 0)
    def _():
        start_all(gather(0, 0))

    @pl.when(i + 1 < n_tiles)
    def _():
        start_all(gather(i + 1, (i + 1) % 2))

    buf = i % 2
    for cp in gather(i, buf):
        cp.wait()
    w = w_ref[...]
    h = h_ref[...] + w[:, 0:1] * ybuf[buf, 0] + w[:, 1:2] * ybuf[buf, 1]
    o_ref[...] = h * lax.rsqrt(jnp.mean(h * h, axis=-1, keepdims=True) + NORM_EPS) * gf_ref[...]


def _combine(h, y, slot1, slot2, w, gf):
    t, d = h.shape
    tm = _tile(t, 128)
    grid_spec = pltpu.PrefetchScalarGridSpec(
        num_scalar_prefetch=2,
        grid=(t // tm,),
        in_specs=[
            pl.BlockSpec((tm, d), lambda i, s1, s2: (i, 0)),
            pl.BlockSpec((tm, 2), lambda i, s1, s2: (i, 0)),
            pl.BlockSpec((1, d), lambda i, s1, s2: (0, 0)),
            pl.BlockSpec(memory_space=pl.ANY),
        ],
        out_specs=pl.BlockSpec((tm, d), lambda i, s1, s2: (i, 0)),
        scratch_shapes=[pltpu.VMEM((2, 2, tm, d), F32), pltpu.SemaphoreType.DMA((2,))],
    )
    return pl.pallas_call(
        _combine_kernel,
        out_shape=jax.ShapeDtypeStruct((t, d), F32),
        grid_spec=grid_spec,
        compiler_params=_params(("arbitrary",)),
        name="combine",
    )(slot1, slot2, h, w, gf.reshape(1, d), y)


def _layer(h, batch, seq, norm1_g, w_in, conv_dw_w, conv_dw_b, conv_ln_g, conv_ln_b, w_conv_out, w_attn_out,
           gate_b, w_out, norm2_g, w_rg, b_rg, w_re, b_re, w_exp_gate, w_exp_up, w_exp_down):
    t, d = h.shape
    c = w_conv_out.shape[0]
    aw = w_attn_out.shape[0]
    n_heads = aw // HEAD_DIM
    c1 = 2 * c
    c2 = c1 + 3 * aw

    w_in_b = w_in.astype(BF16)
    z, u = _norm_glu_proj(h, norm1_g, w_in_b, c)
    qkv = _proj_call(_proj_kernel, u, w_in_b, c1, [], 3 * aw, BF16, "qkv_proj")
    gates = _proj_call(_gate_proj_kernel, u, w_in_b, c2, [gate_b.reshape(1, 2 * d)], 2 * d, BF16, "gate_proj")

    conv = _conv_branch(z, seq, conv_dw_w, conv_dw_b, conv_ln_g, conv_ln_b)
    attn = _moba(qkv, batch, seq, n_heads)

    wr_t = jnp.zeros((ROUTER_ROWS, d), F32).at[:N_GROUPS].set(w_rg.T).at[N_GROUPS:N_GROUPS + N_EXPERTS].set(w_re.T)
    b_r = jnp.zeros((ROUTER_ROWS,), F32).at[:N_GROUPS].set(b_rg).at[N_GROUPS:N_GROUPS + N_EXPERTS].set(b_re)
    h_mid, route, counts = _mix(conv, attn, gates, h, w_conv_out.astype(BF16), w_attn_out.astype(BF16),
                                w_out.astype(BF16), norm2_g, wr_t.astype(BF16), b_r.reshape(ROUTER_ROWS, 1))

    n_assign = 2 * t
    n_blocks = -(-n_assign // MOE_BLOCK) + N_EXPERTS
    counts = counts[:, 0].astype(jnp.int32)
    padded = (counts + MOE_BLOCK - 1) // MOE_BLOCK * MOE_BLOCK
    pad_end = jnp.cumsum(padded)
    pad_start = pad_end - padded
    ids = jnp.arange(N_EXPERTS, dtype=jnp.int32)

    def slot_of(expert, rank):
        start = jnp.sum(jnp.where(ids[:, None] == expert[None, :].astype(jnp.int32), pad_start[:, None], 0), axis=0)
        return start + rank.astype(jnp.int32)

    slot1 = slot_of(route[0], route[4])
    slot2 = slot_of(route[1], route[5])
    blk_start = jnp.arange(n_blocks, dtype=jnp.int32) * MOE_BLOCK
    blk_expert = jnp.minimum(jnp.sum(pad_end[None, :] <= blk_start[:, None], axis=1), N_EXPERTS - 1).astype(jnp.int32)
    n_valid = (pad_end[-1:] // MOE_BLOCK).astype(jnp.int32)
    nonempty = counts > 0
    later = jnp.logical_and(ids[None, :] > ids[:, None], nonempty[None, :])
    next_expert = jnp.min(jnp.where(later, ids[None, :], N_EXPERTS), axis=1).astype(jnp.int32)
    parity = (jnp.sum(jnp.logical_and(ids[None, :] < ids[:, None], nonempty[None, :]), axis=1) % 2).astype(jnp.int32)

    xs = _dispatch(h_mid, slot1, slot2, pad_start + counts, pad_end, n_valid, n_blocks)
    y = _experts(xs, norm2_g, w_exp_gate, w_exp_up, w_exp_down, blk_expert, n_valid, next_expert, parity)
    weights = jnp.stack([route[2], route[3]], axis=1)
    return h_mid, y, slot1, slot2, weights


def kernel(x, norm1_g, w_in, conv_dw_w, conv_dw_b, conv_ln_g, conv_ln_b, w_conv_out, w_attn_out, gate_b, w_out,
           norm2_g, w_router_group, b_router_group, w_router_expert, b_router_expert, w_exp_gate, w_exp_up,
           w_exp_down, norm_f_g):
    b, s, d = x.shape
    assert norm1_g.shape[0] == 1, "single-layer stack: the combine kernel fuses the final RMSNorm"
    h_mid, y, slot1, slot2, weights = _layer(
        x.reshape(b * s, d), b, s, norm1_g[0], w_in[0], conv_dw_w[0], conv_dw_b[0], conv_ln_g[0], conv_ln_b[0],
        w_conv_out[0], w_attn_out[0], gate_b[0], w_out[0], norm2_g[0], w_router_group[0],
        b_router_group[0], w_router_expert[0], b_router_expert[0], w_exp_gate[0], w_exp_up[0], w_exp_down[0])
    return _combine(h_mid, y, slot1, slot2, weights, norm_f_g).reshape(b, s, d)
```

```python
import functools
import math

import jax
import jax.numpy as jnp
from jax import lax
from jax.experimental import pallas as pl
from jax.experimental.pallas import tpu as pltpu

F32 = jnp.float32
BF16 = jnp.bfloat16

NORM_EPS = 1e-6
NEG_BIG = -1e30
MASKED = -2e30
LOG2_E = 1.4426950408889634
MOBA_GROUP = 4
MOBA_HEADS_PER_STEP = 4
MOBA_BIAS_PIECES = 3
MOBA_SUM_ROWS = 8
CONV_TAPS = 31
HEAD_DIM = 128
MOBA_BLOCK = 256
MOBA_TOPK = 3
N_GROUPS = 8
EXPERTS_PER_GROUP = 8
N_EXPERTS = N_GROUPS * EXPERTS_PER_GROUP
MOE_BLOCK = 256
ROUTER_ROWS = 128
CONV_HALO = 32
SUBLANES = 8
V7X_VMEM_LIMIT_BYTES = 56 * 1024 * 1024

NT_DIMS = (((1,), (1,)), ((), ()))


def _params(semantics, flags=None):
    return pltpu.CompilerParams(dimension_semantics=semantics, vmem_limit_bytes=V7X_VMEM_LIMIT_BYTES, flags=flags)


def _tile(n, want):
    t = min(n, want)
    while n % t:
        t //= 2
    return t


def _norm_glu_proj_kernel(x_ref, g_ref, wv_ref, wg_ref, z_ref, u_ref):
    x = x_ref[...]
    u = (x * lax.rsqrt(jnp.mean(x * x, axis=-1, keepdims=True) + NORM_EPS) * g_ref[...]).astype(u_ref.dtype)
    u_ref[...] = u
    val = jnp.dot(u, wv_ref[...], preferred_element_type=F32)
    gate = jnp.dot(u, wg_ref[...], preferred_element_type=F32)
    z_ref[...] = val * jax.nn.sigmoid(gate)


def _norm_glu_proj(x, g, w_in, c):
    t, d = x.shape
    tm = _tile(t, 512)
    const = lambda shape, col: pl.BlockSpec(shape, lambda i: (0, col), pipeline_mode=pl.Buffered(1))
    return pl.pallas_call(
        _norm_glu_proj_kernel,
        out_shape=(jax.ShapeDtypeStruct((t, c), F32), jax.ShapeDtypeStruct((t, d), BF16)),
        grid=(t // tm,),
        in_specs=[pl.BlockSpec((tm, d), lambda i: (i, 0)), const((1, d), 0), const((d, c), 0), const((d, c), 1)],
        out_specs=(pl.BlockSpec((tm, c), lambda i: (i, 0)), pl.BlockSpec((tm, d), lambda i: (i, 0))),
        compiler_params=_params(("parallel",)),
        name="norm_glu_proj",
    )(x, g.reshape(1, d), w_in, w_in)


def _proj_kernel(u_ref, w_ref, o_ref):
    o_ref[...] = jnp.dot(u_ref[...], w_ref[...], preferred_element_type=F32).astype(o_ref.dtype)


def _gate_proj_kernel(u_ref, w_ref, b_ref, o_ref):
    y = jnp.dot(u_ref[...], w_ref[...], preferred_element_type=F32) + b_ref[...]
    o_ref[...] = jax.nn.sigmoid(y).astype(o_ref.dtype)


def _proj_call(kernel, u, w_in, col0, extra, n_out, out_dtype, name):
    t, k = u.shape
    tm = _tile(t, 1024)
    tn = math.gcd(_tile(n_out, 1024), col0)
    first = col0 // tn
    in_specs = [pl.BlockSpec((tm, k), lambda i, j: (i, 0)),
                pl.BlockSpec((k, tn), lambda i, j: (0, first + j))]
    in_specs += [pl.BlockSpec((1, tn), lambda i, j: (0, j)) for _ in extra]
    return pl.pallas_call(
        kernel,
        out_shape=jax.ShapeDtypeStruct((t, n_out), out_dtype),
        grid=(t // tm, n_out // tn),
        in_specs=in_specs,
        out_specs=pl.BlockSpec((tm, tn), lambda i, j: (i, j)),
        compiler_params=_params(("parallel", "parallel")),
        name=name,
    )(u, w_in, *extra)


def _conv_kernel(z_ref, halo_ref, w_ref, b_ref, g_ref, beta_ref, o_ref, ext_ref, acc_ref, *, tiles_per_seq):
    i = pl.program_id(0)
    ts, c = z_ref.shape
    seq_start = (i % tiles_per_seq) == 0
    n_lt = c // 128
    span = ts + CONV_HALO - SUBLANES
    for lt in range(n_lt):
        ls = slice(lt * 128, (lt + 1) * 128)
        ext_ref[0, lt, 0:CONV_HALO, :] = jnp.where(seq_start, 0.0, halo_ref[:, ls])
        ext_ref[0, lt, CONV_HALO:, :] = z_ref[:, ls]
        for s in range(1, SUBLANES):
            ext_ref[s, lt, 0:span, :] = ext_ref[0, lt, s:s + span, :]
    rows = 64
    first = CONV_HALO - (CONV_TAPS - 1)
    n_rc = ts // rows

    def chunk(ci, carry):
        lt = ci // n_rc
        r0 = pl.multiple_of((ci % n_rc) * rows, rows)
        accs = [jnp.broadcast_to(b_ref[lt], (rows, 128)), jnp.zeros((rows, 128), F32)]
        for tap in range(CONV_TAPS):
            s = (first + tap) % SUBLANES
            base = pl.multiple_of(r0 + (first + tap - s), SUBLANES)
            accs[tap % 2] = accs[tap % 2] + ext_ref[s, lt, pl.ds(base, rows), :] * w_ref[lt, tap:tap + 1, :]
        acc_ref[lt, pl.ds(r0, rows), :] = accs[0] + accs[1]
        return carry

    lax.fori_loop(0, n_lt * n_rc, chunk, 0)
    y = jnp.concatenate([acc_ref[lt] for lt in range(n_lt)], axis=1)
    mu = jnp.mean(y, axis=-1, keepdims=True)
    yc = y - mu
    var = jnp.mean(yc * yc, axis=-1, keepdims=True)
    yn = yc * lax.rsqrt(var + NORM_EPS) * g_ref[...] + beta_ref[...]
    o_ref[...] = (yn * jax.nn.sigmoid(yn)).astype(o_ref.dtype)


def _conv_branch(z, seq, w_dw, b_dw, ln_g, ln_b):
    t, c = z.shape
    ts = _tile(seq, 256)
    hb = ts // CONV_HALO
    kernel = functools.partial(_conv_kernel, tiles_per_seq=seq // ts)
    row = lambda v: v.reshape(1, c)
    n_lt = c // 128
    w_lt = w_dw.reshape(CONV_TAPS, n_lt, 128).transpose(1, 0, 2)
    b_lt = b_dw.reshape(n_lt, 1, 128)
    return pl.pallas_call(
        kernel,
        out_shape=jax.ShapeDtypeStruct((t, c), BF16),
        grid=(t // ts,),
        in_specs=[
            pl.BlockSpec((ts, c), lambda i: (i, 0)),
            pl.BlockSpec((CONV_HALO, c), lambda i: (jnp.maximum(i * hb - 1, 0), 0)),
            pl.BlockSpec((n_lt, CONV_TAPS, 128), lambda i: (0, 0, 0)),
            pl.BlockSpec((n_lt, 1, 128), lambda i: (0, 0, 0)),
            pl.BlockSpec((1, c), lambda i: (0, 0)),
            pl.BlockSpec((1, c), lambda i: (0, 0)),
        ],
        out_specs=pl.BlockSpec((ts, c), lambda i: (i, 0)),
        scratch_shapes=[pltpu.VMEM((SUBLANES, n_lt, ts + CONV_HALO, 128), F32), pltpu.VMEM((n_lt, ts, 128), F32)],
        compiler_params=_params(("parallel",)),
        name="conv",
    )(z, z, w_lt, b_lt, row(ln_g), row(ln_b))


def _moba_kernel(q_ref, k_ref, v_ref, o_ref, kmean_ref, kaug_ref, vt_ref, sel_ref, mask_ref,
                 *, n_blk, n_heads, hp, scale):
    head0 = pl.program_id(1) * hp
    qb = pl.program_id(2)
    blk = MOBA_BLOCK
    dh = HEAD_DIM
    lanes = [slice(h * dh, (h + 1) * dh) for h in range(hp)]
    slope2 = [jnp.exp2(-8.0 * (head0 + h + 1).astype(F32) / n_heads) * LOG2_E for h in range(hp)]

    @pl.when(qb == 0)
    def _():
        k_local = lax.broadcasted_iota(jnp.int32, (blk, dh), 0).astype(F32)
        col = lax.broadcasted_iota(jnp.int32, (blk, dh), 1)
        extras = []
        for h in range(hp):
            rest = slope2[h] * k_local
            extra = jnp.zeros((blk, dh), F32)
            for piece in range(MOBA_BIAS_PIECES):
                part = rest.astype(BF16).astype(F32)
                extra = jnp.where(col == piece, part, extra)
                rest = rest - part
            extras.append(extra.astype(BF16))

        def per_block(kb, carry):
            st = pl.multiple_of(kb * blk, blk)
            for h in range(hp):
                k = k_ref[pl.ds(st, blk), lanes[h]]
                kmean_ref[h, pl.ds(kb, 1), :] = jnp.mean(k.astype(F32), axis=0, keepdims=True)
                kaug_ref[h, pl.ds(st, blk), 0:dh] = k
                kaug_ref[h, pl.ds(st, blk), dh:2 * dh] = extras[h]
                vt_ref[h, 0:dh, pl.ds(st, blk)] = v_ref[pl.ds(st, blk), lanes[h]].astype(F32).T.astype(BF16)
            return carry
        lax.fori_loop(0, n_blk, per_block, 0)
        ones_row = lax.broadcasted_iota(jnp.int32, (MOBA_SUM_ROWS, vt_ref.shape[2]), 0) == 0
        for h in range(hp):
            vt_ref[h, dh:dh + MOBA_SUM_ROWS, :] = jnp.where(ones_row, 1.0, 0.0).astype(BF16)
        key_i = lax.broadcasted_iota(jnp.int32, (blk, blk), 0)
        query_i = lax.broadcasted_iota(jnp.int32, (blk, blk), 1)
        mask_ref[...] = jnp.where(query_i >= key_i, 0.0, MASKED)

    rows = lax.broadcasted_iota(jnp.int32, (n_blk, blk), 0)
    ones_cols = jnp.where(lax.broadcasted_iota(jnp.int32, (blk, dh), 1) < MOBA_BIAS_PIECES, 1.0, 0.0).astype(BF16)
    q_aug = []
    for h in range(hp):
        q = q_ref[:, lanes[h]]
        gate = lax.dot_general(kmean_ref[h].astype(BF16), q, NT_DIMS, preferred_element_type=F32)
        gate = jnp.where(rows < qb, gate, -jnp.inf)
        sel = jnp.zeros((n_blk, blk), F32)
        for _ in range(MOBA_TOPK):
            top = jnp.max(gate, axis=0, keepdims=True)
            idx = jnp.min(jnp.where(gate == top, rows, n_blk), axis=0, keepdims=True)
            pick = rows == idx
            sel = jnp.where(pick, jnp.where(top > -jnp.inf, 1.0, sel), sel)
            gate = jnp.where(pick, -jnp.inf, gate)
        sel_ref[h] = sel
        qs = (q.astype(F32) * (scale * LOG2_E)).astype(BF16)
        q_aug.append(jnp.concatenate([qs, ones_cols], axis=1))

    def block_scores(h, st):
        return lax.dot_general(kaug_ref[h, pl.ds(st, blk), :], q_aug[h], NT_DIMS, preferred_element_type=F32)

    own_st = pl.multiple_of(qb * blk, blk)
    own = [block_scores(h, own_st) + mask_ref[...] for h in range(hp)]
    carry = []
    for h in range(hp):
        m = jnp.max(own[h], axis=0, keepdims=True)
        p = jnp.exp2(own[h] - m).astype(BF16)
        carry.append((m, jnp.dot(vt_ref[h, :, pl.ds(own_st, blk)], p, preferred_element_type=F32)))

    def per_group(j, carry):
        starts = [pl.multiple_of((j * MOBA_GROUP + g) * blk, blk) for g in range(MOBA_GROUP)]
        chains = [(h, j * MOBA_GROUP + g, starts[g]) for h in range(hp) for g in range(MOBA_GROUP)]
        scores = [block_scores(h, st) for h, _, st in chains]
        stats = []
        for (h, kb, _), s in zip(chains, scores):
            mb = jnp.max(s, axis=0, keepdims=True)
            p = jnp.exp2(s - mb).astype(BF16)
            block_term = slope2[h] * ((kb - qb) * blk).astype(F32)
            mb = jnp.where(sel_ref[h, pl.ds(kb, 1), :] > 0.5, mb + block_term, MASKED)
            stats.append((mb, p))
        pvs = [jnp.dot(vt_ref[h, :, pl.ds(st, blk)], p, preferred_element_type=F32)
               for (h, _, st), (_, p) in zip(chains, stats)]
        out = []
        for h in range(hp):
            m, acc = carry[h]
            mine = [(mb, pv) for (hh, _, _), (mb, _), pv in zip(chains, stats, pvs) if hh == h]
            m_new = m
            for mb, _ in mine:
                m_new = jnp.maximum(m_new, mb)
            acc = jnp.exp2(m - m_new) * acc
            for mb, pv in mine:
                acc = acc + jnp.exp2(mb - m_new) * pv
            out.append((m_new, acc))
        return tuple(out)

    n_groups = (qb + MOBA_GROUP - 1) // MOBA_GROUP
    final = lax.fori_loop(0, n_groups, per_group, tuple(carry))
    for h in range(hp):
        _, acc = final[h]
        o_ref[:, lanes[h]] = (acc[0:dh] / acc[dh:dh + 1]).T.astype(o_ref.dtype)


def _moba(qkv, batch, seq, n_heads):
    t = qkv.shape[0]
    blk = MOBA_BLOCK
    n_blk = seq // blk
    hp = _tile(n_heads, MOBA_HEADS_PER_STEP)
    assert n_blk % MOBA_GROUP == 0
    kernel = functools.partial(_moba_kernel, n_blk=n_blk, n_heads=n_heads, hp=hp, scale=HEAD_DIM ** -0.5)
    wide = hp * HEAD_DIM
    hsteps = n_heads // hp
    return pl.pallas_call(
        kernel,
        out_shape=jax.ShapeDtypeStruct((t, n_heads * HEAD_DIM), BF16),
        grid=(batch, hsteps, n_blk),
        in_specs=[
            pl.BlockSpec((blk, wide), lambda b, h, i: (b * n_blk + i, h)),
            pl.BlockSpec((seq, wide), lambda b, h, i: (b, hsteps + h), pipeline_mode=pl.Buffered(1)),
            pl.BlockSpec((seq, wide), lambda b, h, i: (b, 2 * hsteps + h), pipeline_mode=pl.Buffered(1)),
        ],
        out_specs=pl.BlockSpec((blk, wide), lambda b, h, i: (b * n_blk + i, h)),
        scratch_shapes=[
            pltpu.VMEM((hp, n_blk, HEAD_DIM), F32),
            pltpu.VMEM((hp, seq, 2 * HEAD_DIM), BF16),
            pltpu.VMEM((hp, HEAD_DIM + MOBA_SUM_ROWS, seq), BF16),
            pltpu.VMEM((hp, n_blk, blk), F32),
            pltpu.VMEM((blk, blk), F32),
        ],
        compiler_params=_params(("parallel", "parallel", "arbitrary")),
        name="moba",
    )(qkv, qkv, qkv)


def _mix_kernel(c_ref, a_ref, gates_ref, x_ref, wpw_ref, wao_ref, wout_ref, g2_ref, wrt_ref, br_ref,
                h_ref, route_ref, cnt_ref, carry_ref):
    i = pl.program_id(0)
    tm, d = x_ref.shape

    @pl.when(i == 0)
    def _():
        carry_ref[...] = jnp.zeros_like(carry_ref)

    yc = jnp.dot(c_ref[...], wpw_ref[...], preferred_element_type=F32)
    ya = jnp.dot(a_ref[...], wao_ref[...], preferred_element_type=F32)
    merged = gates_ref[:, :d].astype(F32) * yc + gates_ref[:, d:].astype(F32) * ya
    h = x_ref[...] + jnp.dot(merged.astype(BF16), wout_ref[...], preferred_element_type=F32)
    h_ref[...] = h
    hn = h * lax.rsqrt(jnp.mean(h * h, axis=-1, keepdims=True) + NORM_EPS) * g2_ref[...]
    logits = lax.dot_general(wrt_ref[...], hn.astype(BF16), NT_DIMS, preferred_element_type=F32) + br_ref[...]

    iota8 = lax.broadcasted_iota(jnp.int32, (N_GROUPS, tm), 0)
    gl = logits[0:N_GROUPS]
    gmax = jnp.max(gl, axis=0, keepdims=True)
    g_p = 1.0 / jnp.sum(jnp.exp(gl - gmax), axis=0, keepdims=True)
    g_idx = jnp.min(jnp.where(gl == gmax, iota8, N_GROUPS), axis=0, keepdims=True)
    esel = jnp.zeros((EXPERTS_PER_GROUP, tm), F32)
    for g in range(N_GROUPS):
        lo = N_GROUPS + g * EXPERTS_PER_GROUP
        esel = jnp.where(g_idx == g, logits[lo:lo + EXPERTS_PER_GROUP], esel)
    v1 = jnp.max(esel, axis=0, keepdims=True)
    i1 = jnp.min(jnp.where(esel == v1, iota8, EXPERTS_PER_GROUP), axis=0, keepdims=True)
    rest = jnp.where(iota8 == i1, -jnp.inf, esel)
    v2 = jnp.max(rest, axis=0, keepdims=True)
    i2 = jnp.min(jnp.where(rest == v2, iota8, EXPERTS_PER_GROUP), axis=0, keepdims=True)
    r = jnp.exp(v2 - v1)
    w1 = g_p / (1.0 + r)
    w2 = g_p * r / (1.0 + r)
    e1 = g_idx * EXPERTS_PER_GROUP + i1
    e2 = g_idx * EXPERTS_PER_GROUP + i2

    iota_e = lax.broadcasted_iota(jnp.int32, (N_EXPERTS, tm), 0)
    oh1 = iota_e == e1
    oh2 = iota_e == e2
    onehot = jnp.concatenate([jnp.where(oh1, 1.0, 0.0), jnp.where(oh2, 1.0, 0.0)], axis=1)
    before = (lax.broadcasted_iota(jnp.int32, (2 * tm, 2 * tm), 0)
              < lax.broadcasted_iota(jnp.int32, (2 * tm, 2 * tm), 1))
    prefix = jnp.dot(onehot.astype(BF16), jnp.where(before, 1.0, 0.0).astype(BF16), preferred_element_type=F32)
    carry = carry_ref[...]
    base = prefix + carry[:, 0:1]
    rank1 = jnp.sum(jnp.where(oh1, base[:, :tm], 0.0), axis=0, keepdims=True)
    rank2 = jnp.sum(jnp.where(oh2, base[:, tm:], 0.0), axis=0, keepdims=True)
    total = carry + jnp.sum(onehot, axis=1, keepdims=True)
    carry_ref[...] = total
    cnt_ref[...] = total

    route_ref[0:1, :] = e1.astype(F32)
    route_ref[1:2, :] = e2.astype(F32)
    route_ref[2:3, :] = w1
    route_ref[3:4, :] = w2
    route_ref[4:5, :] = rank1
    route_ref[5:6, :] = rank2
    route_ref[6:8, :] = jnp.zeros((2, tm), F32)


def _mix(c, a, gates, x, w_pw, w_ao, w_out, g2, wr_t, b_r):
    t, d = x.shape
    cw = c.shape[1]
    aw = a.shape[1]
    tm = _tile(t, 256)
    const = lambda shape: pl.BlockSpec(shape, lambda i: (0, 0), pipeline_mode=pl.Buffered(1))
    return pl.pallas_call(
        _mix_kernel,
        out_shape=(
            jax.ShapeDtypeStruct((t, d), F32),
            jax.ShapeDtypeStruct((8, t), F32),
            jax.ShapeDtypeStruct((N_EXPERTS, 128), F32),
        ),
        grid=(t // tm,),
        in_specs=[
            pl.BlockSpec((tm, cw), lambda i: (i, 0)),
            pl.BlockSpec((tm, aw), lambda i: (i, 0)),
            pl.BlockSpec((tm, 2 * d), lambda i: (i, 0)),
            pl.BlockSpec((tm, d), lambda i: (i, 0)),
            const((cw, d)),
            const((aw, d)),
            const((d, d)),
            const((1, d)),
            const((ROUTER_ROWS, d)),
            const((ROUTER_ROWS, 1)),
        ],
        out_specs=(
            pl.BlockSpec((tm, d), lambda i: (i, 0)),
            pl.BlockSpec((8, tm), lambda i: (0, i)),
            pl.BlockSpec((N_EXPERTS, 128), lambda i: (0, 0)),
        ),
        scratch_shapes=[pltpu.VMEM((N_EXPERTS, 128), F32)],
        compiler_params=_params(("arbitrary",)),
        name="mix",
    )(c, a, gates, x, w_pw, w_ao, w_out, g2.reshape(1, d), wr_t, b_r)


def _row_copy(src_hbm, row, dst_vmem, dst_row, sem):
    return pltpu.make_async_copy(src_hbm.at[pl.ds(row, 1), :], dst_vmem.at[pl.ds(dst_row, 1), :], sem)


def _dispatch_kernel(slot1_ref, slot2_ref, pad_lo_ref, pad_hi_ref, n_valid_ref, h_ref, xs_hbm, hbuf, sems,
                     *, n_blocks):
    i = pl.program_id(0)
    n_steps = pl.num_programs(0)
    tm = h_ref.shape[0]
    buf = i % 2
    hbuf[buf] = h_ref[...]

    def token_copies(step, b):
        cps = []
        for r in range(tm):
            t = step * tm + r
            src = hbuf.at[b, pl.ds(r, 1), :]
            cps.append(pltpu.make_async_copy(src, xs_hbm.at[pl.ds(slot1_ref[t], 1), :], sems.at[b]))
            cps.append(pltpu.make_async_copy(src, xs_hbm.at[pl.ds(slot2_ref[t], 1), :], sems.at[b]))
        return cps

    def pad_copies(e, todo):
        lo = pad_lo_ref[e]
        aligned = (lo + SUBLANES - 1) // SUBLANES * SUBLANES

        def single(s, carry):
            todo(pltpu.make_async_copy(hbuf.at[buf, pl.ds(0, 1), :], xs_hbm.at[pl.ds(s, 1), :], sems.at[2]))
            return carry
        lax.fori_loop(lo, aligned, single, 0)
        n = pad_hi_ref[e] - aligned
        size = tm // 2
        while size >= SUBLANES:
            @pl.when((n & size) != 0)
            def _(size=size):
                first = pl.multiple_of(aligned + (n & ~(2 * size - 1)), SUBLANES)
                todo(pltpu.make_async_copy(hbuf.at[buf, pl.ds(0, size), :], xs_hbm.at[pl.ds(first, size), :],
                                           sems.at[2]))
            size //= 2

    def tail_copy(block):
        return pltpu.make_async_copy(hbuf.at[buf], xs_hbm.at[pl.ds(block * tm, tm), :], sems.at[3])

    for k, cp in enumerate(token_copies(i, buf)):
        cp.start(priority=k % 2)

    @pl.when(i > 0)
    def _():
        for cp in token_copies(i - 1, 1 - buf):
            cp.wait()

    @pl.when(i == n_steps - 1)
    def _():
        def start_tail(b, carry):
            tail_copy(b).start()
            return carry
        lax.fori_loop(n_valid_ref[0], n_blocks, start_tail, 0)
        for e in range(N_EXPERTS):
            pad_copies(e, lambda cp: cp.start())
        for cp in token_copies(i, buf):
            cp.wait()
        for e in range(N_EXPERTS):
            pad_copies(e, lambda cp: cp.wait())

        def wait_tail(b, carry):
            tail_copy(b).wait()
            return carry
        lax.fori_loop(n_valid_ref[0], n_blocks, wait_tail, 0)


def _dispatch(h, slot1, slot2, pad_lo, pad_hi, n_valid, n_blocks):
    t, d = h.shape
    tm = MOE_BLOCK
    assert t % tm == 0
    grid_spec = pltpu.PrefetchScalarGridSpec(
        num_scalar_prefetch=5,
        grid=(t // tm,),
        in_specs=[pl.BlockSpec((tm, d), lambda i, *_: (i, 0))],
        out_specs=pl.BlockSpec(memory_space=pl.ANY),
        scratch_shapes=[pltpu.VMEM((2, tm, d), h.dtype), pltpu.SemaphoreType.DMA((4,))],
    )
    return pl.pallas_call(
        functools.partial(_dispatch_kernel, n_blocks=n_blocks),
        out_shape=jax.ShapeDtypeStruct((n_blocks * MOE_BLOCK, d), h.dtype),
        grid_spec=grid_spec,
        compiler_params=_params(("arbitrary",)),
        name="dispatch",
    )(slot1, slot2, pad_lo, pad_hi, n_valid, h)


def _experts_kernel(blk_expert_ref, n_valid_ref, next_expert_ref, parity_ref,
                    x_ref, g2_ref, wg_hbm, wu_hbm, wd_hbm, y_ref,
                    wg_f, wu_f, wd_f, wsem, wg_b, wu_b, wd_b):
    n = pl.program_id(0)
    n_valid = n_valid_ref[0]

    def weight_copies(e, slot):
        return [pltpu.make_async_copy(wg_hbm.at[e], wg_f.at[slot], wsem.at[slot]),
                pltpu.make_async_copy(wu_hbm.at[e], wu_f.at[slot], wsem.at[slot]),
                pltpu.make_async_copy(wd_hbm.at[e], wd_f.at[slot], wsem.at[slot])]

    def start_weights(e, slot):
        for cp in weight_copies(e, slot):
            cp.start()

    @pl.when(n == 0)
    def _():
        start_weights(blk_expert_ref[0], 0)

    @pl.when(n >= n_valid)
    def _():
        y_ref[...] = jnp.zeros_like(y_ref)

    @pl.when(n < n_valid)
    def _():
        e = blk_expert_ref[n]
        wslot = parity_ref[e]

        @pl.when(jnp.logical_or(n == 0, blk_expert_ref[jnp.maximum(n - 1, 0)] != e))
        def _():
            nxt = next_expert_ref[e]

            @pl.when(nxt < N_EXPERTS)
            def _():
                start_weights(nxt, 1 - wslot)

            for cp in weight_copies(e, wslot):
                cp.wait()
            wg_b[...] = wg_f[wslot].astype(BF16)
            wu_b[...] = wu_f[wslot].astype(BF16)
            wd_b[...] = wd_f[wslot].astype(BF16)

        x = x_ref[...]
        xn = x * lax.rsqrt(jnp.mean(x * x, axis=-1, keepdims=True) + NORM_EPS) * g2_ref[...]
        xb = xn.astype(BF16)
        gate = jnp.dot(xb, wg_b[...], preferred_element_type=F32)
        up = jnp.dot(xb, wu_b[...], preferred_element_type=F32)
        hidden = (gate * jax.nn.sigmoid(gate) * up).astype(BF16)
        y_ref[...] = jnp.dot(hidden, wd_b[...], preferred_element_type=F32)


def _experts(xs, g2, w_g, w_u, w_d, blk_expert, n_valid, next_expert, parity):
    d = xs.shape[1]
    de = w_g.shape[2]
    n_blocks = blk_expert.shape[0]
    rows = MOE_BLOCK
    grid_spec = pltpu.PrefetchScalarGridSpec(
        num_scalar_prefetch=4,
        grid=(n_blocks,),
        in_specs=[
            pl.BlockSpec((rows, d), lambda n, be, nv, *_: (jnp.minimum(n, nv[0] - 1), 0)),
            pl.BlockSpec((1, d), lambda n, *_: (0, 0)),
            pl.BlockSpec(memory_space=pl.ANY),
            pl.BlockSpec(memory_space=pl.ANY),
            pl.BlockSpec(memory_space=pl.ANY),
        ],
        out_specs=pl.BlockSpec((rows, d), lambda n, *_: (n, 0)),
        scratch_shapes=[
            pltpu.VMEM((2, d, de), F32),
            pltpu.VMEM((2, d, de), F32),
            pltpu.VMEM((2, de, d), F32),
            pltpu.SemaphoreType.DMA((2,)),
            pltpu.VMEM((d, de), BF16),
            pltpu.VMEM((d, de), BF16),
            pltpu.VMEM((de, d), BF16),
        ],
    )
    return pl.pallas_call(
        _experts_kernel,
        out_shape=jax.ShapeDtypeStruct((n_blocks * rows, d), F32),
        grid_spec=grid_spec,
        compiler_params=_params(("arbitrary",)),
        name="experts",
    )(blk_expert, n_valid, next_expert, parity, xs, g2.reshape(1, d), w_g, w_u, w_d)


def _combine_kernel(slot1_ref, slot2_ref, h_ref, w_ref, gf_ref, y_hbm, o_ref, ybuf, sems):
    i = pl.program_id(0)
    n_tiles = pl.num_programs(0)
    tm = h_ref.shape[0]

    def gather(tile, buf):
        cps = []
        for r in range(tm):
            cps.append(_row_copy(y_hbm, slot1_ref[tile * tm + r], ybuf.at[buf, 0], r, sems.at[buf]))
            cps.append(_row_copy(y_hbm, slot2_ref[tile * tm + r], ybuf.at[buf, 1], r, sems.at[buf]))
        return cps

    def start_all(copies):
        for k, cp in enumerate(copies):
            cp.start(priority=k % 2)

    @pl.when(i == 0)
    def _():
        start_all(gather(0, 0))

    @pl.when(i + 1 < n_tiles)
    def _():
        start_all(gather(i + 1, (i + 1) % 2))

    buf = i % 2
    for cp in gather(i, buf):
        cp.wait()
    w = w_ref[...]
    h = h_ref[...] + w[:, 0:1] * ybuf[buf, 0] + w[:, 1:2] * ybuf[buf, 1]
    o_ref[...] = h * lax.rsqrt(jnp.mean(h * h, axis=-1, keepdims=True) + NORM_EPS) * gf_ref[...]


def _combine(h, y, slot1, slot2, w, gf):
    t, d = h.shape
    tm = _tile(t, 256)
    grid_spec = pltpu.PrefetchScalarGridSpec(
        num_scalar_prefetch=2,
        grid=(t // tm,),
        in_specs=[
            pl.BlockSpec((tm, d), lambda i, s1, s2: (i, 0)),
            pl.BlockSpec((tm, 2), lambda i, s1, s2: (i, 0)),
            pl.BlockSpec((1, d), lambda i, s1, s2: (0, 0)),
            pl.BlockSpec(memory_space=pl.ANY),
        ],
        out_specs=pl.BlockSpec((tm, d), lambda i, s1, s2: (i, 0)),
        scratch_shapes=[pltpu.VMEM((2, 2, tm, d), F32), pltpu.SemaphoreType.DMA((2,))],
    )
    return pl.pallas_call(
        _combine_kernel,
        out_shape=jax.ShapeDtypeStruct((t, d), F32),
        grid_spec=grid_spec,
        compiler_params=_params(("arbitrary",)),
        name="combine",
    )(slot1, slot2, h, w, gf.reshape(1, d), y)


def _layer(h, batch, seq, norm1_g, w_in, conv_dw_w, conv_dw_b, conv_ln_g, conv_ln_b, w_conv_out, w_attn_out,
           gate_b, w_out, norm2_g, w_rg, b_rg, w_re, b_re, w_exp_gate, w_exp_up, w_exp_down):
    t, d = h.shape
    c = w_conv_out.shape[0]
    aw = w_attn_out.shape[0]
    n_heads = aw // HEAD_DIM
    c1 = 2 * c
    c2 = c1 + 3 * aw

    w_in_b = w_in.astype(BF16)
    z, u = _norm_glu_proj(h, norm1_g, w_in_b, c)
    qkv = _proj_call(_proj_kernel, u, w_in_b, c1, [], 3 * aw, BF16, "qkv_proj")
    gates = _proj_call(_gate_proj_kernel, u, w_in_b, c2, [gate_b.reshape(1, 2 * d)], 2 * d, BF16, "gate_proj")

    conv = _conv_branch(z, seq, conv_dw_w, conv_dw_b, conv_ln_g, conv_ln_b)
    attn = _moba(qkv, batch, seq, n_heads)

    wr_t = jnp.zeros((ROUTER_ROWS, d), F32).at[:N_GROUPS].set(w_rg.T).at[N_GROUPS:N_GROUPS + N_EXPERTS].set(w_re.T)
    b_r = jnp.zeros((ROUTER_ROWS,), F32).at[:N_GROUPS].set(b_rg).at[N_GROUPS:N_GROUPS + N_EXPERTS].set(b_re)
    h_mid, route, counts = _mix(conv, attn, gates, h, w_conv_out.astype(BF16), w_attn_out.astype(BF16),
                                w_out.astype(BF16), norm2_g, wr_t.astype(BF16), b_r.reshape(ROUTER_ROWS, 1))

    n_assign = 2 * t
    n_blocks = -(-n_assign // MOE_BLOCK) + N_EXPERTS
    counts = counts[:, 0].astype(jnp.int32)
    padded = (counts + MOE_BLOCK - 1) // MOE_BLOCK * MOE_BLOCK
    pad_end = jnp.cumsum(padded)
    pad_start = pad_end - padded
    ids = jnp.arange(N_EXPERTS, dtype=jnp.int32)

    def slot_of(expert, rank):
        start = jnp.sum(jnp.where(ids[:, None] == expert[None, :].astype(jnp.int32), pad_start[:, None], 0), axis=0)
        return start + rank.astype(jnp.int32)

    slot1 = slot_of(route[0], route[4])
    slot2 = slot_of(route[1], route[5])
    blk_start = jnp.arange(n_blocks, dtype=jnp.int32) * MOE_BLOCK
    blk_expert = jnp.minimum(jnp.sum(pad_end[None, :] <= blk_start[:, None], axis=1), N_EXPERTS - 1).astype(jnp.int32)
    n_valid = (pad_end[-1:] // MOE_BLOCK).astype(jnp.int32)
    nonempty = counts > 0
    later = jnp.logical_and(ids[None, :] > ids[:, None], nonempty[None, :])
    next_expert = jnp.min(jnp.where(later, ids[None, :], N_EXPERTS), axis=1).astype(jnp.int32)
    parity = (jnp.sum(jnp.logical_and(ids[None, :] < ids[:, None], nonempty[None, :]), axis=1) % 2).astype(jnp.int32)

    xs = _dispatch(h_mid, slot1, slot2, pad_start + counts, pad_end, n_valid, n_blocks)
    y = _experts(xs, norm2_g, w_exp_gate, w_exp_up, w_exp_down, blk_expert, n_valid, next_expert, parity)
    weights = jnp.stack([route[2], route[3]], axis=1)
    return h_mid, y, slot1, slot2, weights


def kernel(x, norm1_g, w_in, conv_dw_w, conv_dw_b, conv_ln_g, conv_ln_b, w_conv_out, w_attn_out, gate_b, w_out,
           norm2_g, w_router_group, b_router_group, w_router_expert, b_router_expert, w_exp_gate, w_exp_up,
           w_exp_down, norm_f_g):
    b, s, d = x.shape
    assert norm1_g.shape[0] == 1, "single-layer stack: the combine kernel fuses the final RMSNorm"
    h_mid, y, slot1, slot2, weights = _layer(
        x.reshape(b * s, d), b, s, norm1_g[0], w_in[0], conv_dw_w[0], conv_dw_b[0], conv_ln_g[0], conv_ln_b[0],
        w_conv_out[0], w_attn_out[0], gate_b[0], w_out[0], norm2_g[0], w_router_group[0],
        b_router_group[0], w_router_expert[0], b_router_expert[0], w_exp_gate[0], w_exp_up[0], w_exp_down[0])
    return _combine(h_mid, y, slot1, slot2, weights, norm_f_g).reshape(b, s, d)
```

```python
import functools
import math

import jax
import jax.numpy as jnp
from jax import lax
from jax.experimental import pallas as pl
from jax.experimental.pallas import tpu as pltpu

F32 = jnp.float32
BF16 = jnp.bfloat16

NORM_EPS = 1e-6
NEG_BIG = -1e30
MASKED = -2e30
LOG2_E = 1.4426950408889634
MOBA_GROUP = 4
MOBA_HEADS_PER_STEP = 4
MOBA_BIAS_PIECES = 3
MOBA_SUM_ROWS = 8
CONV_TAPS = 31
HEAD_DIM = 128
MOBA_BLOCK = 256
MOBA_TOPK = 3
N_GROUPS = 8
EXPERTS_PER_GROUP = 8
N_EXPERTS = N_GROUPS * EXPERTS_PER_GROUP
MOE_BLOCK = 256
ROUTER_ROWS = 128
CONV_HALO = 32
SUBLANES = 8
V7X_VMEM_LIMIT_BYTES = 56 * 1024 * 1024

NT_DIMS = (((1,), (1,)), ((), ()))


def _params(semantics, flags=None):
    return pltpu.CompilerParams(dimension_semantics=semantics, vmem_limit_bytes=V7X_VMEM_LIMIT_BYTES, flags=flags)


def _tile(n, want):
    t = min(n, want)
    while n % t:
        t //= 2
    return t


def _norm_glu_proj_kernel(x_ref, g_ref, wv_ref, wg_ref, z_ref, u_ref):
    x = x_ref[...]
    u = (x * lax.rsqrt(jnp.mean(x * x, axis=-1, keepdims=True) + NORM_EPS) * g_ref[...]).astype(u_ref.dtype)
    u_ref[...] = u
    val = jnp.dot(u, wv_ref[...], preferred_element_type=F32)
    gate = jnp.dot(u, wg_ref[...], preferred_element_type=F32)
    z_ref[...] = val * jax.nn.sigmoid(gate)


def _norm_glu_proj(x, g, w_in, c):
    t, d = x.shape
    tm = _tile(t, 512)
    const = lambda shape, col: pl.BlockSpec(shape, lambda i: (0, col), pipeline_mode=pl.Buffered(1))
    return pl.pallas_call(
        _norm_glu_proj_kernel,
        out_shape=(jax.ShapeDtypeStruct((t, c), F32), jax.ShapeDtypeStruct((t, d), BF16)),
        grid=(t // tm,),
        in_specs=[pl.BlockSpec((tm, d), lambda i: (i, 0)), const((1, d), 0), const((d, c), 0), const((d, c), 1)],
        out_specs=(pl.BlockSpec((tm, c), lambda i: (i, 0)), pl.BlockSpec((tm, d), lambda i: (i, 0))),
        compiler_params=_params(("parallel",)),
        name="norm_glu_proj",
    )(x, g.reshape(1, d), w_in, w_in)


def _proj_kernel(u_ref, w_ref, o_ref):
    o_ref[...] = jnp.dot(u_ref[...], w_ref[...], preferred_element_type=F32).astype(o_ref.dtype)


def _gate_proj_kernel(u_ref, w_ref, b_ref, o_ref):
    y = jnp.dot(u_ref[...], w_ref[...], preferred_element_type=F32) + b_ref[...]
    o_ref[...] = jax.nn.sigmoid(y).astype(o_ref.dtype)


def _proj_call(kernel, u, w_in, col0, extra, n_out, out_dtype, name):
    t, k = u.shape
    tm = _tile(t, 1024)
    tn = math.gcd(_tile(n_out, 1024), col0)
    first = col0 // tn
    in_specs = [pl.BlockSpec((tm, k), lambda i, j: (i, 0)),
                pl.BlockSpec((k, tn), lambda i, j: (0, first + j))]
    in_specs += [pl.BlockSpec((1, tn), lambda i, j: (0, j)) for _ in extra]
    return pl.pallas_call(
        kernel,
        out_shape=jax.ShapeDtypeStruct((t, n_out), out_dtype),
        grid=(t // tm, n_out // tn),
        in_specs=in_specs,
        out_specs=pl.BlockSpec((tm, tn), lambda i, j: (i, j)),
        compiler_params=_params(("parallel", "parallel")),
        name=name,
    )(u, w_in, *extra)


def _conv_kernel(z_ref, halo_ref, w_ref, b_ref, g_ref, beta_ref, o_ref, ext_ref, acc_ref, *, tiles_per_seq):
    i = pl.program_id(0)
    ts, c = z_ref.shape
    seq_start = (i % tiles_per_seq) == 0
    n_lt = c // 128
    span = ts + CONV_HALO - SUBLANES
    for lt in range(n_lt):
        ls = slice(lt * 128, (lt + 1) * 128)
        ext_ref[0, lt, 0:CONV_HALO, :] = jnp.where(seq_start, 0.0, halo_ref[:, ls])
        ext_ref[0, lt, CONV_HALO:, :] = z_ref[:, ls]
        for s in range(1, SUBLANES):
            ext_ref[s, lt, 0:span, :] = ext_ref[0, lt, s:s + span, :]
    rows = 64
    first = CONV_HALO - (CONV_TAPS - 1)
    n_rc = ts // rows

    def chunk(ci, carry):
        lt = ci // n_rc
        r0 = pl.multiple_of((ci % n_rc) * rows, rows)
        accs = [jnp.broadcast_to(b_ref[lt], (rows, 128)), jnp.zeros((rows, 128), F32)]
        for tap in range(CONV_TAPS):
            s = (first + tap) % SUBLANES
            base = pl.multiple_of(r0 + (first + tap - s), SUBLANES)
            accs[tap % 2] = accs[tap % 2] + ext_ref[s, lt, pl.ds(base, rows), :] * w_ref[lt, tap:tap + 1, :]
        acc_ref[lt, pl.ds(r0, rows), :] = accs[0] + accs[1]
        return carry

    lax.fori_loop(0, n_lt * n_rc, chunk, 0)
    y = jnp.concatenate([acc_ref[lt] for lt in range(n_lt)], axis=1)
    mu = jnp.mean(y, axis=-1, keepdims=True)
    yc = y - mu
    var = jnp.mean(yc * yc, axis=-1, keepdims=True)
    yn = yc * lax.rsqrt(var + NORM_EPS) * g_ref[...] + beta_ref[...]
    o_ref[...] = (yn * jax.nn.sigmoid(yn)).astype(o_ref.dtype)


def _conv_branch(z, seq, w_dw, b_dw, ln_g, ln_b):
    t, c = z.shape
    ts = _tile(seq, 256)
    hb = ts // CONV_HALO
    kernel = functools.partial(_conv_kernel, tiles_per_seq=seq // ts)
    row = lambda v: v.reshape(1, c)
    n_lt = c // 128
    w_lt = w_dw.reshape(CONV_TAPS, n_lt, 128).transpose(1, 0, 2)
    b_lt = b_dw.reshape(n_lt, 1, 128)
    return pl.pallas_call(
        kernel,
        out_shape=jax.ShapeDtypeStruct((t, c), BF16),
        grid=(t // ts,),
        in_specs=[
            pl.BlockSpec((ts, c), lambda i: (i, 0)),
            pl.BlockSpec((CONV_HALO, c), lambda i: (jnp.maximum(i * hb - 1, 0), 0)),
            pl.BlockSpec((n_lt, CONV_TAPS, 128), lambda i: (0, 0, 0)),
            pl.BlockSpec((n_lt, 1, 128), lambda i: (0, 0, 0)),
            pl.BlockSpec((1, c), lambda i: (0, 0)),
            pl.BlockSpec((1, c), lambda i: (0, 0)),
        ],
        out_specs=pl.BlockSpec((ts, c), lambda i: (i, 0)),
        scratch_shapes=[pltpu.VMEM((SUBLANES, n_lt, ts + CONV_HALO, 128), F32), pltpu.VMEM((n_lt, ts, 128), F32)],
        compiler_params=_params(("parallel",)),
        name="conv",
    )(z, z, w_lt, b_lt, row(ln_g), row(ln_b))


def _moba_kernel(q_ref, k_ref, v_ref, o_ref, kmean_ref, kaug_ref, vt_ref, sel_ref, mask_ref,
                 *, n_blk, n_heads, hp, scale):
    head0 = pl.program_id(1) * hp
    qb = pl.program_id(2)
    blk = MOBA_BLOCK
    dh = HEAD_DIM
    lanes = [slice(h * dh, (h + 1) * dh) for h in range(hp)]
    slope2 = [jnp.exp2(-8.0 * (head0 + h + 1).astype(F32) / n_heads) * LOG2_E for h in range(hp)]

    @pl.when(qb == 0)
    def _():
        k_local = lax.broadcasted_iota(jnp.int32, (blk, dh), 0).astype(F32)
        col = lax.broadcasted_iota(jnp.int32, (blk, dh), 1)
        extras = []
        for h in range(hp):
            rest = slope2[h] * k_local
            extra = jnp.zeros((blk, dh), F32)
            for piece in range(MOBA_BIAS_PIECES):
                part = rest.astype(BF16).astype(F32)
                extra = jnp.where(col == piece, part, extra)
                rest = rest - part
            extras.append(extra.astype(BF16))

        def per_block(kb, carry):
            st = pl.multiple_of(kb * blk, blk)
            for h in range(hp):
                k = k_ref[pl.ds(st, blk), lanes[h]]
                kmean_ref[h, pl.ds(kb, 1), :] = jnp.mean(k.astype(F32), axis=0, keepdims=True)
                kaug_ref[h, pl.ds(st, blk), 0:dh] = k
                kaug_ref[h, pl.ds(st, blk), dh:2 * dh] = extras[h]
                vt_ref[h, 0:dh, pl.ds(st, blk)] = v_ref[pl.ds(st, blk), lanes[h]].astype(F32).T.astype(BF16)
            return carry
        lax.fori_loop(0, n_blk, per_block, 0)
        ones_row = lax.broadcasted_iota(jnp.int32, (MOBA_SUM_ROWS, vt_ref.shape[2]), 0) == 0
        for h in range(hp):
            vt_ref[h, dh:dh + MOBA_SUM_ROWS, :] = jnp.where(ones_row, 1.0, 0.0).astype(BF16)
        key_i = lax.broadcasted_iota(jnp.int32, (blk, blk), 0)
        query_i = lax.broadcasted_iota(jnp.int32, (blk, blk), 1)
        mask_ref[...] = jnp.where(query_i >= key_i, 0.0, MASKED)

    rows = lax.broadcasted_iota(jnp.int32, (n_blk, blk), 0)
    ones_cols = jnp.where(lax.broadcasted_iota(jnp.int32, (blk, dh), 1) < MOBA_BIAS_PIECES, 1.0, 0.0).astype(BF16)
    q_aug = []
    for h in range(hp):
        q = q_ref[:, lanes[h]]
        gate = lax.dot_general(kmean_ref[h].astype(BF16), q, NT_DIMS, preferred_element_type=F32)
        gate = jnp.where(rows < qb, gate, -jnp.inf)
        sel = jnp.zeros((n_blk, blk), F32)
        for _ in range(MOBA_TOPK):
            top = jnp.max(gate, axis=0, keepdims=True)
            idx = jnp.min(jnp.where(gate == top, rows, n_blk), axis=0, keepdims=True)
            pick = rows == idx
            sel = jnp.where(pick, jnp.where(top > -jnp.inf, 1.0, sel), sel)
            gate = jnp.where(pick, -jnp.inf, gate)
        sel_ref[h] = sel
        qs = (q.astype(F32) * (scale * LOG2_E)).astype(BF16)
        q_aug.append(jnp.concatenate([qs, ones_cols], axis=1))

    def block_scores(h, st):
        return lax.dot_general(kaug_ref[h, pl.ds(st, blk), :], q_aug[h], NT_DIMS, preferred_element_type=F32)

    own_st = pl.multiple_of(qb * blk, blk)
    own = [block_scores(h, own_st) + mask_ref[...] for h in range(hp)]
    carry = []
    for h in range(hp):
        m = jnp.max(own[h], axis=0, keepdims=True)
        p = jnp.exp2(own[h] - m).astype(BF16)
        carry.append((m, jnp.dot(vt_ref[h, :, pl.ds(own_st, blk)], p, preferred_element_type=F32)))

    def per_group(j, carry):
        starts = [pl.multiple_of((j * MOBA_GROUP + g) * blk, blk) for g in range(MOBA_GROUP)]
        chains = [(h, j * MOBA_GROUP + g, starts[g]) for h in range(hp) for g in range(MOBA_GROUP)]
        scores = [block_scores(h, st) for h, _, st in chains]
        stats = []
        for (h, kb, _), s in zip(chains, scores):
            mb = jnp.max(s, axis=0, keepdims=True)
            p = jnp.exp2(s - mb).astype(BF16)
            block_term = slope2[h] * ((kb - qb) * blk).astype(F32)
            mb = jnp.where(sel_ref[h, pl.ds(kb, 1), :] > 0.5, mb + block_term, MASKED)
            stats.append((mb, p))
        pvs = [jnp.dot(vt_ref[h, :, pl.ds(st, blk)], p, preferred_element_type=F32)
               for (h, _, st), (_, p) in zip(chains, stats)]
        out = []
        for h in range(hp):
            m, acc = carry[h]
            mine = [(mb, pv) for (hh, _, _), (mb, _), pv in zip(chains, stats, pvs) if hh == h]
            m_new = m
            for mb, _ in mine:
                m_new = jnp.maximum(m_new, mb)
            acc = jnp.exp2(m - m_new) * acc
            for mb, pv in mine:
                acc = acc + jnp.exp2(mb - m_new) * pv
            out.append((m_new, acc))
        return tuple(out)

    n_groups = (qb + MOBA_GROUP - 1) // MOBA_GROUP
    final = lax.fori_loop(0, n_groups, per_group, tuple(carry))
    for h in range(hp):
        _, acc = final[h]
        o_ref[:, lanes[h]] = (acc[0:dh] / acc[dh:dh + 1]).T.astype(o_ref.dtype)


def _moba(qkv, batch, seq, n_heads):
    t = qkv.shape[0]
    blk = MOBA_BLOCK
    n_blk = seq // blk
    hp = _tile(n_heads, MOBA_HEADS_PER_STEP)
    assert n_blk % MOBA_GROUP == 0
    kernel = functools.partial(_moba_kernel, n_blk=n_blk, n_heads=n_heads, hp=hp, scale=HEAD_DIM ** -0.5)
    wide = hp * HEAD_DIM
    hsteps = n_heads // hp
    return pl.pallas_call(
        kernel,
        out_shape=jax.ShapeDtypeStruct((t, n_heads * HEAD_DIM), BF16),
        grid=(batch, hsteps, n_blk),
        in_specs=[
            pl.BlockSpec((blk, wide), lambda b, h, i: (b * n_blk + i, h)),
            pl.BlockSpec((seq, wide), lambda b, h, i: (b, hsteps + h), pipeline_mode=pl.Buffered(1)),
            pl.BlockSpec((seq, wide), lambda b, h, i: (b, 2 * hsteps + h), pipeline_mode=pl.Buffered(1)),
        ],
        out_specs=pl.BlockSpec((blk, wide), lambda b, h, i: (b * n_blk + i, h)),
        scratch_shapes=[
            pltpu.VMEM((hp, n_blk, HEAD_DIM), F32),
            pltpu.VMEM((hp, seq, 2 * HEAD_DIM), BF16),
            pltpu.VMEM((hp, HEAD_DIM + MOBA_SUM_ROWS, seq), BF16),
            pltpu.VMEM((hp, n_blk, blk), F32),
            pltpu.VMEM((blk, blk), F32),
        ],
        compiler_params=_params(("parallel", "parallel", "arbitrary")),
        name="moba",
    )(qkv, qkv, qkv)


def _mix_kernel(c_ref, a_ref, gates_ref, x_ref, wpw_ref, wao_ref, wout_ref, g2_ref, wrt_ref, br_ref,
                h_ref, route_ref, cnt_ref, carry_ref):
    i = pl.program_id(0)
    tm, d = x_ref.shape

    @pl.when(i == 0)
    def _():
        carry_ref[...] = jnp.zeros_like(carry_ref)

    yc = jnp.dot(c_ref[...], wpw_ref[...], preferred_element_type=F32)
    ya = jnp.dot(a_ref[...], wao_ref[...], preferred_element_type=F32)
    merged = gates_ref[:, :d].astype(F32) * yc + gates_ref[:, d:].astype(F32) * ya
    h = x_ref[...] + jnp.dot(merged.astype(BF16), wout_ref[...], preferred_element_type=F32)
    h_ref[...] = h
    hn = h * lax.rsqrt(jnp.mean(h * h, axis=-1, keepdims=True) + NORM_EPS) * g2_ref[...]
    logits = lax.dot_general(wrt_ref[...], hn.astype(BF16), NT_DIMS, preferred_element_type=F32) + br_ref[...]

    iota8 = lax.broadcasted_iota(jnp.int32, (N_GROUPS, tm), 0)
    gl = logits[0:N_GROUPS]
    gmax = jnp.max(gl, axis=0, keepdims=True)
    g_p = 1.0 / jnp.sum(jnp.exp(gl - gmax), axis=0, keepdims=True)
    g_idx = jnp.min(jnp.where(gl == gmax, iota8, N_GROUPS), axis=0, keepdims=True)
    esel = jnp.zeros((EXPERTS_PER_GROUP, tm), F32)
    for g in range(N_GROUPS):
        lo = N_GROUPS + g * EXPERTS_PER_GROUP
        esel = jnp.where(g_idx == g, logits[lo:lo + EXPERTS_PER_GROUP], esel)
    v1 = jnp.max(esel, axis=0, keepdims=True)
    i1 = jnp.min(jnp.where(esel == v1, iota8, EXPERTS_PER_GROUP), axis=0, keepdims=True)
    rest = jnp.where(iota8 == i1, -jnp.inf, esel)
    v2 = jnp.max(rest, axis=0, keepdims=True)
    i2 = jnp.min(jnp.where(rest == v2, iota8, EXPERTS_PER_GROUP), axis=0, keepdims=True)
    r = jnp.exp(v2 - v1)
    w1 = g_p / (1.0 + r)
    w2 = g_p * r / (1.0 + r)
    e1 = g_idx * EXPERTS_PER_GROUP + i1
    e2 = g_idx * EXPERTS_PER_GROUP + i2

    iota_e = lax.broadcasted_iota(jnp.int32, (N_EXPERTS, tm), 0)
    oh1 = iota_e == e1
    oh2 = iota_e == e2
    onehot = jnp.concatenate([jnp.where(oh1, 1.0, 0.0), jnp.where(oh2, 1.0, 0.0)], axis=1)
    before = (lax.broadcasted_iota(jnp.int32, (2 * tm, 2 * tm), 0)
              < lax.broadcasted_iota(jnp.int32, (2 * tm, 2 * tm), 1))
    prefix = jnp.dot(onehot.astype(BF16), jnp.where(before, 1.0, 0.0).astype(BF16), preferred_element_type=F32)
    carry = carry_ref[...]
    base = prefix + carry[:, 0:1]
    rank1 = jnp.sum(jnp.where(oh1, base[:, :tm], 0.0), axis=0, keepdims=True)
    rank2 = jnp.sum(jnp.where(oh2, base[:, tm:], 0.0), axis=0, keepdims=True)
    total = carry + jnp.sum(onehot, axis=1, keepdims=True)
    carry_ref[...] = total
    cnt_ref[...] = total

    route_ref[0:1, :] = e1.astype(F32)
    route_ref[1:2, :] = e2.astype(F32)
    route_ref[2:3, :] = w1
    route_ref[3:4, :] = w2
    route_ref[4:5, :] = rank1
    route_ref[5:6, :] = rank2
    route_ref[6:8, :] = jnp.zeros((2, tm), F32)


def _mix(c, a, gates, x, w_pw, w_ao, w_out, g2, wr_t, b_r):
    t, d = x.shape
    cw = c.shape[1]
    aw = a.shape[1]
    tm = _tile(t, 256)
    const = lambda shape: pl.BlockSpec(shape, lambda i: (0, 0), pipeline_mode=pl.Buffered(1))
    return pl.pallas_call(
        _mix_kernel,
        out_shape=(
            jax.ShapeDtypeStruct((t, d), F32),
            jax.ShapeDtypeStruct((8, t), F32),
            jax.ShapeDtypeStruct((N_EXPERTS, 128), F32),
        ),
        grid=(t // tm,),
        in_specs=[
            pl.BlockSpec((tm, cw), lambda i: (i, 0)),
            pl.BlockSpec((tm, aw), lambda i: (i, 0)),
            pl.BlockSpec((tm, 2 * d), lambda i: (i, 0)),
            pl.BlockSpec((tm, d), lambda i: (i, 0)),
            const((cw, d)),
            const((aw, d)),
            const((d, d)),
            const((1, d)),
            const((ROUTER_ROWS, d)),
            const((ROUTER_ROWS, 1)),
        ],
        out_specs=(
            pl.BlockSpec((tm, d), lambda i: (i, 0)),
            pl.BlockSpec((8, tm), lambda i: (0, i)),
            pl.BlockSpec((N_EXPERTS, 128), lambda i: (0, 0)),
        ),
        scratch_shapes=[pltpu.VMEM((N_EXPERTS, 128), F32)],
        compiler_params=_params(("arbitrary",)),
        name="mix",
    )(c, a, gates, x, w_pw, w_ao, w_out, g2.reshape(1, d), wr_t, b_r)


def _row_copy(src_hbm, row, dst_vmem, dst_row, sem):
    return pltpu.make_async_copy(src_hbm.at[pl.ds(row, 1), :], dst_vmem.at[pl.ds(dst_row, 1), :], sem)


def _dispatch_kernel(slot1_ref, slot2_ref, pad_lo_ref, pad_hi_ref, n_valid_ref, h_ref, xs_hbm, hbuf, sems,
                     *, n_blocks):
    i = pl.program_id(0)
    n_steps = pl.num_programs(0)
    tm = h_ref.shape[0]
    buf = i % 2
    hbuf[buf] = h_ref[...]

    def token_copies(step, b):
        cps = []
        for r in range(tm):
            t = step * tm + r
            src = hbuf.at[b, pl.ds(r, 1), :]
            cps.append(pltpu.make_async_copy(src, xs_hbm.at[pl.ds(slot1_ref[t], 1), :], sems.at[b]))
            cps.append(pltpu.make_async_copy(src, xs_hbm.at[pl.ds(slot2_ref[t], 1), :], sems.at[b]))
        return cps

    def pad_copies(e, todo):
        lo = pad_lo_ref[e]
        aligned = (lo + SUBLANES - 1) // SUBLANES * SUBLANES

        def single(s, carry):
            todo(pltpu.make_async_copy(hbuf.at[buf, pl.ds(0, 1), :], xs_hbm.at[pl.ds(s, 1), :], sems.at[2]))
            return carry
        lax.fori_loop(lo, aligned, single, 0)
        n = pad_hi_ref[e] - aligned
        size = tm // 2
        while size >= SUBLANES:
            @pl.when((n & size) != 0)
            def _(size=size):
                first = pl.multiple_of(aligned + (n & ~(2 * size - 1)), SUBLANES)
                todo(pltpu.make_async_copy(hbuf.at[buf, pl.ds(0, size), :], xs_hbm.at[pl.ds(first, size), :],
                                           sems.at[2]))
            size //= 2

    def tail_copy(block):
        return pltpu.make_async_copy(hbuf.at[buf], xs_hbm.at[pl.ds(block * tm, tm), :], sems.at[3])

    for k, cp in enumerate(token_copies(i, buf)):
        cp.start(priority=k % 2)

    @pl.when(i > 0)
    def _():
        for cp in token_copies(i - 1, 1 - buf):
            cp.wait()

    @pl.when(i == n_steps - 1)
    def _():
        def start_tail(b, carry):
            tail_copy(b).start()
            return carry
        lax.fori_loop(n_valid_ref[0], n_blocks, start_tail, 0)
        for e in range(N_EXPERTS):
            pad_copies(e, lambda cp: cp.start())
        for cp in token_copies(i, buf):
            cp.wait()
        for e in range(N_EXPERTS):
            pad_copies(e, lambda cp: cp.wait())

        def wait_tail(b, carry):
            tail_copy(b).wait()
            return carry
        lax.fori_loop(n_valid_ref[0], n_blocks, wait_tail, 0)


def _dispatch(h, slot1, slot2, pad_lo, pad_hi, n_valid, n_blocks):
    t, d = h.shape
    tm = MOE_BLOCK
    assert t % tm == 0
    grid_spec = pltpu.PrefetchScalarGridSpec(
        num_scalar_prefetch=5,
        grid=(t // tm,),
        in_specs=[pl.BlockSpec((tm, d), lambda i, *_: (i, 0))],
        out_specs=pl.BlockSpec(memory_space=pl.ANY),
        scratch_shapes=[pltpu.VMEM((2, tm, d), h.dtype), pltpu.SemaphoreType.DMA((4,))],
    )
    return pl.pallas_call(
        functools.partial(_dispatch_kernel, n_blocks=n_blocks),
        out_shape=jax.ShapeDtypeStruct((n_blocks * MOE_BLOCK, d), h.dtype),
        grid_spec=grid_spec,
        compiler_params=_params(("arbitrary",)),
        name="dispatch",
    )(slot1, slot2, pad_lo, pad_hi, n_valid, h)


def _experts_kernel(blk_expert_ref, n_valid_ref, next_expert_ref, parity_ref,
                    x_ref, g2_ref, wg_hbm, wu_hbm, wd_hbm, y_ref,
                    wg_f, wu_f, wd_f, wsem, wg_b, wu_b, wd_b):
    n = pl.program_id(0)
    n_valid = n_valid_ref[0]

    def weight_copies(e, slot):
        return [pltpu.make_async_copy(wg_hbm.at[e], wg_f.at[slot], wsem.at[slot]),
                pltpu.make_async_copy(wu_hbm.at[e], wu_f.at[slot], wsem.at[slot]),
                pltpu.make_async_copy(wd_hbm.at[e], wd_f.at[slot], wsem.at[slot])]

    def start_weights(e, slot):
        for cp in weight_copies(e, slot):
            cp.start(priority=1)

    @pl.when(n == 0)
    def _():
        start_weights(blk_expert_ref[0], 0)

    @pl.when(n >= n_valid)
    def _():
        y_ref[...] = jnp.zeros_like(y_ref)

    @pl.when(n < n_valid)
    def _():
        e = blk_expert_ref[n]
        wslot = parity_ref[e]

        @pl.when(jnp.logical_or(n == 0, blk_expert_ref[jnp.maximum(n - 1, 0)] != e))
        def _():
            nxt = next_expert_ref[e]

            @pl.when(nxt < N_EXPERTS)
            def _():
                start_weights(nxt, 1 - wslot)

            for cp in weight_copies(e, wslot):
                cp.wait()
            wg_b[...] = wg_f[wslot].astype(BF16)
            wu_b[...] = wu_f[wslot].astype(BF16)
            wd_b[...] = wd_f[wslot].astype(BF16)

        x = x_ref[...]
        xn = x * lax.rsqrt(jnp.mean(x * x, axis=-1, keepdims=True) + NORM_EPS) * g2_ref[...]
        xb = xn.astype(BF16)
        gate = jnp.dot(xb, wg_b[...], preferred_element_type=F32)
        up = jnp.dot(xb, wu_b[...], preferred_element_type=F32)
        hidden = (gate * jax.nn.sigmoid(gate) * up).astype(BF16)
        y_ref[...] = jnp.dot(hidden, wd_b[...], preferred_element_type=F32)


def _experts(xs, g2, w_g, w_u, w_d, blk_expert, n_valid, next_expert, parity):
    d = xs.shape[1]
    de = w_g.shape[2]
    n_blocks = blk_expert.shape[0]
    rows = MOE_BLOCK
    grid_spec = pltpu.PrefetchScalarGridSpec(
        num_scalar_prefetch=4,
        grid=(n_blocks,),
        in_specs=[
            pl.BlockSpec((rows, d), lambda n, be, nv, *_: (jnp.minimum(n, nv[0] - 1), 0)),
            pl.BlockSpec((1, d), lambda n, *_: (0, 0)),
            pl.BlockSpec(memory_space=pl.ANY),
            pl.BlockSpec(memory_space=pl.ANY),
            pl.BlockSpec(memory_space=pl.ANY),
        ],
        out_specs=pl.BlockSpec((rows, d), lambda n, *_: (n, 0)),
        scratch_shapes=[
            pltpu.VMEM((2, d, de), F32),
            pltpu.VMEM((2, d, de), F32),
            pltpu.VMEM((2, de, d), F32),
            pltpu.SemaphoreType.DMA((2,)),
            pltpu.VMEM((d, de), BF16),
            pltpu.VMEM((d, de), BF16),
            pltpu.VMEM((de, d), BF16),
        ],
    )
    return pl.pallas_call(
        _experts_kernel,
        out_shape=jax.ShapeDtypeStruct((n_blocks * rows, d), F32),
        grid_spec=grid_spec,
        compiler_params=_params(("arbitrary",)),
        name="experts",
    )(blk_expert, n_valid, next_expert, parity, xs, g2.reshape(1, d), w_g, w_u, w_d)


def _combine_kernel(slot1_ref, slot2_ref, h_ref, w_ref, gf_ref, y_hbm, o_ref, ybuf, sems):
    i = pl.program_id(0)
    n_tiles = pl.num_programs(0)
    tm = h_ref.shape[0]

    def gather(tile, buf):
        cps = []
        for r in range(tm):
            cps.append(_row_copy(y_hbm, slot1_ref[tile * tm + r], ybuf.at[buf, 0], r, sems.at[buf]))
            cps.append(_row_copy(y_hbm, slot2_ref[tile * tm + r], ybuf.at[buf, 1], r, sems.at[buf]))
        return cps

    def start_all(copies):
        for k, cp in enumerate(copies):
            cp.start(priority=k % 2)

    @pl.when(i == 0)
    def _():
        start_all(gather(0, 0))

    @pl.when(i + 1 < n_tiles)
    def _():
        start_all(gather(i + 1, (i + 1) % 2))

    buf = i % 2
    for cp in gather(i, buf):
        cp.wait()
    w = w_ref[...]
    h = h_ref[...] + w[:, 0:1] * ybuf[buf, 0] + w[:, 1:2] * ybuf[buf, 1]
    o_ref[...] = h * lax.rsqrt(jnp.mean(h * h, axis=-1, keepdims=True) + NORM_EPS) * gf_ref[...]


def _combine(h, y, slot1, slot2, w, gf):
    t, d = h.shape
    tm = _tile(t, 256)
    grid_spec = pltpu.PrefetchScalarGridSpec(
        num_scalar_prefetch=2,
        grid=(t // tm,),
        in_specs=[
            pl.BlockSpec((tm, d), lambda i, s1, s2: (i, 0)),
            pl.BlockSpec((tm, 2), lambda i, s1, s2: (i, 0)),
            pl.BlockSpec((1, d), lambda i, s1, s2: (0, 0)),
            pl.BlockSpec(memory_space=pl.ANY),
        ],
        out_specs=pl.BlockSpec((tm, d), lambda i, s1, s2: (i, 0)),
        scratch_shapes=[pltpu.VMEM((2, 2, tm, d), F32), pltpu.SemaphoreType.DMA((2,))],
    )
    return pl.pallas_call(
        _combine_kernel,
        out_shape=jax.ShapeDtypeStruct((t, d), F32),
        grid_spec=grid_spec,
        compiler_params=_params(("arbitrary",)),
        name="combine",
    )(slot1, slot2, h, w, gf.reshape(1, d), y)


def _layer(h, batch, seq, norm1_g, w_in, conv_dw_w, conv_dw_b, conv_ln_g, conv_ln_b, w_conv_out, w_attn_out,
           gate_b, w_out, norm2_g, w_rg, b_rg, w_re, b_re, w_exp_gate, w_exp_up, w_exp_down):
    t, d = h.shape
    c = w_conv_out.shape[0]
    aw = w_attn_out.shape[0]
    n_heads = aw // HEAD_DIM
    c1 = 2 * c
    c2 = c1 + 3 * aw

    w_in_b = w_in.astype(BF16)
    z, u = _norm_glu_proj(h, norm1_g, w_in_b, c)
    qkv = _proj_call(_proj_kernel, u, w_in_b, c1, [], 3 * aw, BF16, "qkv_proj")
    gates = _proj_call(_gate_proj_kernel, u, w_in_b, c2, [gate_b.reshape(1, 2 * d)], 2 * d, BF16, "gate_proj")

    conv = _conv_branch(z, seq, conv_dw_w, conv_dw_b, conv_ln_g, conv_ln_b)
    attn = _moba(qkv, batch, seq, n_heads)

    wr_t = jnp.zeros((ROUTER_ROWS, d), F32).at[:N_GROUPS].set(w_rg.T).at[N_GROUPS:N_GROUPS + N_EXPERTS].set(w_re.T)
    b_r = jnp.zeros((ROUTER_ROWS,), F32).at[:N_GROUPS].set(b_rg).at[N_GROUPS:N_GROUPS + N_EXPERTS].set(b_re)
    h_mid, route, counts = _mix(conv, attn, gates, h, w_conv_out.astype(BF16), w_attn_out.astype(BF16),
                                w_out.astype(BF16), norm2_g, wr_t.astype(BF16), b_r.reshape(ROUTER_ROWS, 1))

    n_assign = 2 * t
    n_blocks = -(-n_assign // MOE_BLOCK) + N_EXPERTS
    counts = counts[:, 0].astype(jnp.int32)
    padded = (counts + MOE_BLOCK - 1) // MOE_BLOCK * MOE_BLOCK
    pad_end = jnp.cumsum(padded)
    pad_start = pad_end - padded
    ids = jnp.arange(N_EXPERTS, dtype=jnp.int32)

    def slot_of(expert, rank):
        start = jnp.sum(jnp.where(ids[:, None] == expert[None, :].astype(jnp.int32), pad_start[:, None], 0), axis=0)
        return start + rank.astype(jnp.int32)

    slot1 = slot_of(route[0], route[4])
    slot2 = slot_of(route[1], route[5])
    blk_start = jnp.arange(n_blocks, dtype=jnp.int32) * MOE_BLOCK
    blk_expert = jnp.minimum(jnp.sum(pad_end[None, :] <= blk_start[:, None], axis=1), N_EXPERTS - 1).astype(jnp.int32)
    n_valid = (pad_end[-1:] // MOE_BLOCK).astype(jnp.int32)
    nonempty = counts > 0
    later = jnp.logical_and(ids[None, :] > ids[:, None], nonempty[None, :])
    next_expert = jnp.min(jnp.where(later, ids[None, :], N_EXPERTS), axis=1).astype(jnp.int32)
    parity = (jnp.sum(jnp.logical_and(ids[None, :] < ids[:, None], nonempty[None, :]), axis=1) % 2).astype(jnp.int32)

    xs = _dispatch(h_mid, slot1, slot2, pad_start + counts, pad_end, n_valid, n_blocks)
    y = _experts(xs, norm2_g, w_exp_gate, w_exp_up, w_exp_down, blk_expert, n_valid, next_expert, parity)
    weights = jnp.stack([route[2], route[3]], axis=1)
    return h_mid, y, slot1, slot2, weights


def kernel(x, norm1_g, w_in, conv_dw_w, conv_dw_b, conv_ln_g, conv_ln_b, w_conv_out, w_attn_out, gate_b, w_out,
           norm2_g, w_router_group, b_router_group, w_router_expert, b_router_expert, w_exp_gate, w_exp_up,
           w_exp_down, norm_f_g):
    b, s, d = x.shape
    assert norm1_g.shape[0] == 1, "single-layer stack: the combine kernel fuses the final RMSNorm"
    h_mid, y, slot1, slot2, weights = _layer(
        x.reshape(b * s, d), b, s, norm1_g[0], w_in[0], conv_dw_w[0], conv_dw_b[0], conv_ln_g[0], conv_ln_b[0],
        w_conv_out[0], w_attn_out[0], gate_b[0], w_out[0], norm2_g[0], w_router_group[0],
        b_router_group[0], w_router_expert[0], b_router_expert[0], w_exp_gate[0], w_exp_up[0], w_exp_down[0])
    return _combine(h_mid, y, slot1, slot2, weights, norm_f_g).reshape(b, s, d)
```
